```python
import math
import jax
import jax.numpy as jnp
from jax import lax
import numpy as np

D_MODEL = 1024
BATCH = 8
SEQ = 4096
DEPTH = 2

N_EVEN = (DEPTH + 1) // 2
N_ODD = DEPTH // 2

SB_HEADS = 8
SB_HEAD_DIM = 64
SB_WIDTH = SB_HEADS * SB_HEAD_DIM
Q_BLOCK = 128

HG_HEADS = 4
HG_KEY_DIM = 128
HG_VAL_DIM = 128
HG_KEY_WIDTH = HG_HEADS * HG_KEY_DIM
HG_VAL_WIDTH = HG_HEADS * HG_VAL_DIM
HG_CHUNK = 64

IN_SPLITS = (SB_WIDTH, SB_WIDTH, SB_WIDTH, HG_KEY_WIDTH, HG_KEY_WIDTH, HG_VAL_WIDTH, HG_VAL_WIDTH)
IN_COLS = sum(IN_SPLITS)
MIX_WIDTH = SB_WIDTH + HG_VAL_WIDTH

S5_GROUP = 16
S5_GROUPS = D_MODEL // S5_GROUP
S5_STATE = 64
DT_MIN = 1e-3
DT_MAX = 1e-1

D_FF = -(-8 * D_MODEL // (3 * 256)) * 256
N_MOD = 6
EPS = 1e-6

kernel_name = 'sb_hgrn2_s5_hybrid_adaln'


def rms_norm(x, gain):
    xf = x.astype(jnp.float32)
    y = xf * lax.rsqrt(jnp.mean(xf * xf, axis=-1, keepdims=True) + EPS)
    return (y * gain.astype(jnp.float32)).astype(x.dtype)


def _heads(t, n):
    b, s, w = t.shape
    return t.reshape(b, s, n, w // n).transpose(0, 2, 1, 3)


def _merge(t):
    b, h, s, d = t.shape
    return t.transpose(0, 2, 1, 3).reshape(b, s, h * d)


def stick_breaking_attention(q, k, v):
    b, h, s, dh = q.shape
    nb = s // Q_BLOCK
    q_blocks = q.reshape(b, h, nb, Q_BLOCK, dh).transpose(2, 0, 1, 3, 4)
    key_pos = jnp.arange(s)
    scale = dh ** -0.5

    def one_block(args):
        q_blk, blk = args
        z = jnp.einsum('bhqd,bhkd->bhqk', q_blk, k).astype(jnp.float32) * scale
        q_pos = blk * Q_BLOCK + jnp.arange(Q_BLOCK)
        causal = key_pos[None, :] < q_pos[:, None]
        log_beta = jax.nn.log_sigmoid(z)
        log_keep = jnp.where(causal, jax.nn.log_sigmoid(-z), 0.0)
        tail = lax.cumsum(log_keep, axis=3, reverse=True) - log_keep
        w = jnp.where(causal, jnp.exp(jnp.where(causal, log_beta + tail, 0.0)), 0.0)
        return jnp.einsum('bhqk,bhkd->bhqd', w, v.astype(jnp.float32))

    out = lax.map(one_block, (q_blocks, jnp.arange(nb)))
    return out.transpose(1, 2, 0, 3, 4).reshape(b, h, s, dh)


def hgrn2_chunkwise(q, log_f, k, v):
    b, h, s, dk = q.shape
    dv = v.shape[-1]
    nc = s // HG_CHUNK

    def chunks(t):
        return t.astype(jnp.float32).reshape(b, h, nc, HG_CHUNK, t.shape[-1]).transpose(2, 0, 1, 3, 4)

    tri = jnp.tril(jnp.ones((HG_CHUNK, HG_CHUNK), dtype=bool))

    def step(state, inp):
        q_c, g_c, k_c, v_c = inp
        cum = jnp.cumsum(g_c, axis=2)
        diff = cum[:, :, :, None, :] - cum[:, :, None, :, :]
        decay = jnp.exp(jnp.where(tri[:, :, None], diff, -jnp.inf))
        scores = jnp.einsum('bhtd,bhtsd,bhsd->bhts', q_c, decay, k_c)
        intra = jnp.einsum('bhts,bhsv->bhtv', scores, v_c)
        inter = jnp.einsum('bhtd,bhdv->bhtv', q_c * jnp.exp(cum), state)
        last = cum[:, :, -1:, :]
        k_dec = k_c * jnp.exp(last - cum)
        new_state = state * jnp.exp(last[:, :, 0, :, None]) + jnp.einsum('bhsd,bhsv->bhdv', k_dec, v_c)
        return new_state, intra + inter

    state0 = jnp.zeros((b, h, dk, dv), jnp.float32)
    _, out = lax.scan(step, state0, (chunks(q), chunks(log_f), chunks(k), chunks(v)))
    return out.transpose(1, 2, 0, 3, 4).reshape(b, h, s, dv)


def hybrid_mixer(h, w_in, w_out, lower_bound, hg_gain):
    b, s, _ = h.shape
    proj = h @ w_in
    offs = np.cumsum(IN_SPLITS)[:-1].tolist()
    q_a, k_a, v_a, q_b, f_b, i_b, g_b = jnp.split(proj, offs, axis=-1)
    o_a = stick_breaking_attention(_heads(q_a, SB_HEADS), _heads(k_a, SB_HEADS), _heads(v_a, SB_HEADS))
    f_logit = f_b.astype(jnp.float32)
    forget = lower_bound + (1.0 - lower_bound) * jax.nn.sigmoid(f_logit)
    log_f = jnp.log(forget)
    key_b = (1.0 - lower_bound) * jax.nn.sigmoid(-f_logit)
    o_b = hgrn2_chunkwise(_heads(jax.nn.silu(q_b), HG_HEADS), _heads(log_f, HG_HEADS),
                          _heads(key_b, HG_HEADS), _heads(i_b, HG_HEADS))
    o_b = o_b.transpose(0, 2, 1, 3)
    o_b = o_b * lax.rsqrt(jnp.mean(o_b * o_b, axis=-1, keepdims=True) + EPS)
    o_b = o_b * hg_gain.astype(jnp.float32).reshape(HG_HEADS, HG_VAL_DIM)
    o_b = o_b.reshape(b, s, HG_VAL_WIDTH) * jax.nn.silu(g_b.astype(jnp.float32))
    merged = jnp.concatenate([_merge(o_a), o_b], axis=-1).astype(h.dtype)
    return merged @ w_out


def s5_mixer(h, w_in, lam_re, lam_im, log_dt, b_re, b_im, c_re, c_im, d_skip, w_glu):
    b, s, d = h.shape
    u = (h @ w_in).astype(jnp.float32)
    u_g = u.reshape(b, s, S5_GROUPS, S5_GROUP)
    f32 = jnp.float32
    lr = jnp.minimum(lam_re.astype(f32), -1e-4)
    li = lam_im.astype(f32)
    dt = jnp.exp(log_dt.astype(f32))[:, None]
    mag = jnp.exp(lr * dt)
    a_re = mag * jnp.cos(li * dt)
    a_im = mag * jnp.sin(li * dt)
    e_re = a_re - 1.0
    e_im = a_im
    den = lr * lr + li * li
    z_re = (e_re * lr + e_im * li) / den
    z_im = (e_im * lr - e_re * li) / den
    br = b_re.astype(f32)
    bi = b_im.astype(f32)
    bbar_re = z_re[..., None] * br - z_im[..., None] * bi
    bbar_im = z_re[..., None] * bi + z_im[..., None] * br
    cr = c_re.astype(f32)
    ci = c_im.astype(f32)

    def combine(e1, e2):
        a1r, a1i, b1r, b1i = e1
        a2r, a2i, b2r, b2i = e2
        return (a2r * a1r - a2i * a1i, a2r * a1i + a2i * a1r,
                a2r * b1r - a2i * b1i + b2r, a2r * b1i + a2i * b1r + b2i)

    def one_sequence(u_seq):
        bu_re = jnp.einsum('sgh,gph->sgp', u_seq, bbar_re)
        bu_im = jnp.einsum('sgh,gph->sgp', u_seq, bbar_im)
        ar = jnp.broadcast_to(a_re, bu_re.shape)
        ai = jnp.broadcast_to(a_im, bu_re.shape)
        _, _, x_re, x_im = lax.associative_scan(combine, (ar, ai, bu_re, bu_im), axis=0)
        return jnp.einsum('ghp,sgp->sgh', cr, x_re) - jnp.einsum('ghp,sgp->sgh', ci, x_im)

    y = lax.map(one_sequence, u_g).reshape(b, s, d)
    y = y + d_skip.astype(f32) * u
    y = jax.nn.gelu(y).astype(h.dtype)
    val, gate = jnp.split(y @ w_glu, 2, axis=-1)
    return val * jax.nn.sigmoid(gate)


def swiglu(h, w_in, w_out):
    g, u = jnp.split(h @ w_in, 2, axis=-1)
    return (jax.nn.silu(g) * u) @ w_out


def setup_inputs(seed: int = 0) -> dict:
    key = jax.random.key(seed)
    ks = jax.random.split(key, 24)
    f32 = jnp.float32
    d = D_MODEL

    def nrm(k, shape, scale):
        return jax.random.normal(k, shape, f32) * scale

    x = nrm(ks[0], (BATCH, SEQ, d), 1.0)
    c = nrm(ks[1], (BATCH, d), 1.0)
    norm_mix_g = 1.0 + nrm(ks[2], (DEPTH, d), 0.02)
    norm_ffn_g = 1.0 + nrm(ks[3], (DEPTH, d), 0.02)
    ada_w = nrm(ks[4], (DEPTH, d, N_MOD * d), d ** -0.5)
    ada_b = nrm(ks[5], (DEPTH, N_MOD * d), 0.01)
    ffn_w_in = nrm(ks[6], (DEPTH, d, 2 * D_FF), d ** -0.5)
    ffn_w_out = nrm(ks[7], (DEPTH, D_FF, d), D_FF ** -0.5)
    final_norm_g = 1.0 + nrm(ks[8], (d,), 0.02)
    hy_w_in = nrm(ks[9], (N_EVEN, d, IN_COLS), d ** -0.5)
    hy_w_out = nrm(ks[10], (N_EVEN, MIX_WIDTH, d), MIX_WIDTH ** -0.5)
    hg_norm_g = 1.0 + nrm(ks[11], (N_EVEN, HG_VAL_WIDTH), 0.02)
    hg_lb_logits = nrm(ks[12], (DEPTH + 1, HG_KEY_WIDTH), 0.1)
    s5_w_in = nrm(ks[13], (N_ODD, d, d), d ** -0.5)
    s5_lam_re = -0.5 + nrm(ks[14], (N_ODD, S5_GROUPS, S5_STATE), 0.01)
    s5_lam_im = jnp.pi * jnp.arange(S5_STATE, dtype=f32) + nrm(ks[15], (N_ODD, S5_GROUPS, S5_STATE), 0.01)
    s5_log_dt = jax.random.uniform(ks[16], (N_ODD, S5_GROUPS), f32, math.log(DT_MIN), math.log(DT_MAX))
    s5_b_re = nrm(ks[17], (N_ODD, S5_GROUPS, S5_STATE, S5_GROUP), (2 * S5_GROUP) ** -0.5)
    s5_b_im = nrm(ks[18], (N_ODD, S5_GROUPS, S5_STATE, S5_GROUP), (2 * S5_GROUP) ** -0.5)
    s5_c_re = nrm(ks[19], (N_ODD, S5_GROUPS, S5_GROUP, S5_STATE), 0.5)
    s5_c_im = nrm(ks[20], (N_ODD, S5_GROUPS, S5_GROUP, S5_STATE), 0.5)
    s5_d = nrm(ks[21], (N_ODD, d), 1.0)
    s5_w_glu = nrm(ks[22], (N_ODD, d, 2 * d), d ** -0.5)
    return {'x': x, 'c': c, 'norm_mix_g': norm_mix_g, 'norm_ffn_g': norm_ffn_g,
            'ada_w': ada_w, 'ada_b': ada_b, 'ffn_w_in': ffn_w_in, 'ffn_w_out': ffn_w_out,
            'final_norm_g': final_norm_g, 'hy_w_in': hy_w_in, 'hy_w_out': hy_w_out,
            'hg_norm_g': hg_norm_g, 'hg_lb_logits': hg_lb_logits, 's5_w_in': s5_w_in,
            's5_lam_re': s5_lam_re, 's5_lam_im': s5_lam_im, 's5_log_dt': s5_log_dt,
            's5_b_re': s5_b_re, 's5_b_im': s5_b_im, 's5_c_re': s5_c_re, 's5_c_im': s5_c_im,
            's5_d': s5_d, 's5_w_glu': s5_w_glu}


def reference(x, c, norm_mix_g, norm_ffn_g, ada_w, ada_b, ffn_w_in, ffn_w_out, final_norm_g,
              hy_w_in, hy_w_out, hg_norm_g, hg_lb_logits, s5_w_in, s5_lam_re, s5_lam_im,
              s5_log_dt, s5_b_re, s5_b_im, s5_c_re, s5_c_im, s5_d, s5_w_glu):
    lb_all = jnp.cumsum(jax.nn.softmax(hg_lb_logits.astype(jnp.float32), axis=0), axis=0)
    c_act = jax.nn.silu(c)
    for layer in range(DEPTH):
        mod = (c_act @ ada_w[layer] + ada_b[layer])[:, None, :]
        sh_m, sc_m, g_m, sh_f, sc_f, g_f = jnp.split(mod, N_MOD, axis=-1)
        h = rms_norm(x, norm_mix_g[layer]) * (1.0 + sc_m) + sh_m
        i = layer // 2
        if layer % 2 == 0:
            mix = hybrid_mixer(h, hy_w_in[i], hy_w_out[i], lb_all[layer], hg_norm_g[i])
        else:
            mix = s5_mixer(h, s5_w_in[i], s5_lam_re[i], s5_lam_im[i], s5_log_dt[i], s5_b_re[i],
                           s5_b_im[i], s5_c_re[i], s5_c_im[i], s5_d[i], s5_w_glu[i])
        x = x + g_m * mix
        h = rms_norm(x, norm_ffn_g[layer]) * (1.0 + sc_f) + sh_f
        x = x + g_f * swiglu(h, ffn_w_in[layer], ffn_w_out[layer])
    return rms_norm(x, final_norm_g)
```

```python
import functools
import math

import numpy as np
import jax
import jax.numpy as jnp
from jax import lax
from jax.experimental import pallas as pl
from jax.experimental.pallas import tpu as pltpu

F32 = jnp.float32
BF16 = jnp.bfloat16

EPS = 1e-6
N_MOD = 6
SB_HEADS = 8
SB_HEAD_DIM = 64
HG_HEADS = 4
HG_DIM = 128
HG_CHUNK = 64
S5_GROUP = 16
S5_STATE = 64

LANES = 128
SUBLANES = 8
MIB = 1 << 20

SB_DEAD_LOG_WEIGHT = 88.0

NT_DIMS = (((1,), (1,)), ((), ()))
TN_DIMS = (((0,), (0,)), ((), ()))


def _dot(a, b, dims=None):
    if dims is None:
        return jnp.dot(a, b, preferred_element_type=F32)
    return lax.dot_general(a, b, dims, preferred_element_type=F32)


def _split_bf16(x, n):
    parts = []
    rest = x
    for i in range(n):
        p = rest.astype(BF16)
        parts.append(p)
        if i + 1 < n:
            rest = rest - p.astype(F32)
    return parts


def _sigmoid(x):
    return 1.0 / (1.0 + jnp.exp(-x))


def _silu(x):
    return x * _sigmoid(x)


def _rms_scale(x):
    return x * lax.rsqrt(jnp.mean(x * x, axis=-1, keepdims=True) + EPS)


def _norm_mod(x, gain, scale, shift):
    return _rms_scale(x) * (gain * (1.0 + scale)) + shift


def _params(semantics, vmem_mib):
    return pltpu.CompilerParams(dimension_semantics=semantics, vmem_limit_bytes=vmem_mib * MIB)


def _resident(shape, index_map):
    return pl.BlockSpec(shape, index_map, pipeline_mode=pl.Buffered(1))


def _mod_kernel(c_ref, w_ref, b_ref, o_ref):
    a_hi, a_lo = _split_bf16(_silu(c_ref[...]), 2)
    w_hi, w_lo = _split_bf16(w_ref[0], 2)
    acc = _dot(a_hi, w_hi) + _dot(a_hi, w_lo) + _dot(a_lo, w_hi)
    o_ref[0] = acc + b_ref[0]


def _adaln_mod(c, ada_w, ada_b, tn=1536):
    depth, d, n = ada_w.shape
    b = c.shape[0]
    return pl.pallas_call(
        _mod_kernel,
        grid=(depth, n // tn),
        in_specs=[pl.BlockSpec((b, d), lambda l, j: (0, 0)),
                  pl.BlockSpec((1, d, tn), lambda l, j: (l, 0, j)),
                  pl.BlockSpec((1, 1, tn), lambda l, j: (l, 0, j))],
        out_specs=pl.BlockSpec((1, b, tn), lambda l, j: (l, 0, j)),
        out_shape=jax.ShapeDtypeStruct((depth, b, n), F32),
        compiler_params=_params(("arbitrary", "arbitrary"), 40),
        name="adaln_mod",
    )(c, ada_w, ada_b.reshape(depth, 1, n))


def _normmod_proj_kernel(x_ref, mod_ref, gain_ref, w_ref, *o_refs, splits):
    x = x_ref[0]
    h = _norm_mod(x, gain_ref[...], mod_ref[0, 1:2, :], mod_ref[0, 0:1, :]).astype(BF16)
    for o_ref, (c0, c1) in zip(o_refs, splits):
        y = _dot(h, w_ref[:, c0:c1])
        o_ref[...] = y.reshape(o_ref.shape).astype(o_ref.dtype)


def _hybrid_in_proj(x, mod, gain, w, tm=512):
    b, s, d = x.shape
    n = w.shape[1]
    n_a = 3 * SB_HEADS * SB_HEAD_DIM
    return pl.pallas_call(
        functools.partial(_normmod_proj_kernel, splits=((0, n_a), (n_a, n))),
        grid=(b, s // tm),
        in_specs=[pl.BlockSpec((1, tm, d), lambda bi, i: (bi, i, 0)),
                  pl.BlockSpec((1, N_MOD, d), lambda bi, i: (bi, 0, 0)),
                  pl.BlockSpec((1, d), lambda bi, i: (0, 0)),
                  _resident((d, n), lambda bi, i: (0, 0))],
        out_specs=[pl.BlockSpec((1, tm, n_a), lambda bi, i: (bi, i, 0)),
                   pl.BlockSpec((1, tm, n - n_a), lambda bi, i: (bi, i, 0))],
        out_shape=[jax.ShapeDtypeStruct((b, s, n_a), BF16),
                   jax.ShapeDtypeStruct((b, s, n - n_a), F32)],
        compiler_params=_params(("arbitrary", "arbitrary"), 48),
        name="hybrid_in_proj",
    )(x, mod, gain, w)


def _s5_in_proj(x, mod, gain, w, tm=512):
    b, s, d = x.shape
    return pl.pallas_call(
        functools.partial(_normmod_proj_kernel, splits=((0, d),)),
        grid=(b, s // tm),
        in_specs=[pl.BlockSpec((1, tm, d), lambda bi, i: (bi, i, 0)),
                  pl.BlockSpec((1, N_MOD, d), lambda bi, i: (bi, 0, 0)),
                  pl.BlockSpec((1, d), lambda bi, i: (0, 0)),
                  _resident((d, d), lambda bi, i: (0, 0))],
        out_specs=[pl.BlockSpec((tm, d), lambda bi, i: (i, bi))],
        out_shape=[jax.ShapeDtypeStruct((s, b * d), F32)],
        compiler_params=_params(("arbitrary", "arbitrary"), 40),
        name="s5_in_proj",
    )(x, mod, gain, w)[0]


def _sb_attn_kernel(q_ref, k_ref, v_ref, u2_ref, o_ref, acc_ref, carry_ref, *, tq, tk, dh, scale):
    qi = pl.program_id(2)
    q = q_ref[0]
    lane = lax.broadcasted_iota(jnp.int32, (tq, LANES), 1)
    qs = q * jnp.asarray(scale, q.dtype)
    zero = jnp.zeros_like(qs)
    qq = jnp.concatenate([jnp.where(lane < dh, qs, zero), jnp.where(lane >= dh, qs, zero)], axis=0)
    row = lax.broadcasted_iota(jnp.int32, (2 * tq, tk), 0)
    col = lax.broadcasted_iota(jnp.int32, (2 * tq, tk), 1)
    rel = jnp.where(row >= tq, row - tq, row) - col
    q0 = qi * tq
    u2 = u2_ref[...]
    acc_ref[...] = jnp.zeros_like(acc_ref)
    carry_ref[...] = jnp.zeros_like(carry_ref)

    def live(state):
        j, carry_max = state
        return jnp.logical_and(j >= 0, carry_max > -SB_DEAD_LOG_WEIGHT)

    def key_block(state):
        j, _ = state
        off = pl.multiple_of(j * tk, tk)
        kj = k_ref[0, pl.ds(off, tk), :]
        vj = v_ref[0, pl.ds(off, tk), :]
        z = _dot(qq, kj, NT_DIMS)
        causal = rel > (off - q0)
        log_beta = jnp.minimum(z, 0.0) - jnp.log1p(jnp.exp(-jnp.abs(z)))
        log_keep = jnp.where(causal, log_beta - z, 0.0)
        hi, lo = _split_bf16(log_keep, 2)
        sums = _dot(hi, u2) + _dot(lo, u2)
        carry = carry_ref[...]
        tail = sums[:, :tk] + carry
        w = jnp.where(causal, jnp.exp(log_beta + tail), 0.0)
        acc_ref[...] += _dot(w.astype(BF16), vj)
        carry = carry + sums[:, tk:]
        carry_ref[...] = carry
        return j - 1, jnp.max(carry)

    lax.while_loop(live, key_block, ((q0 + tq - 1) // tk, jnp.float32(0.0)))
    acc = acc_ref[...]
    o_ref[0] = jnp.where(lane < dh, acc[:tq], acc[tq:]).astype(o_ref.dtype)


def _sb_attention(qkv, tq=128):
    b, s, _ = qkv.shape
    tk = LANES
    pairs = SB_HEADS * SB_HEAD_DIM // LANES
    j = np.arange(tk)
    u2 = np.concatenate([(j[:, None] > j[None, :]).astype(np.float32), np.ones((tk, tk), np.float32)], axis=1)
    return pl.pallas_call(
        functools.partial(_sb_attn_kernel, tq=tq, tk=tk, dh=SB_HEAD_DIM, scale=SB_HEAD_DIM ** -0.5),
        grid=(b, pairs, s // tq),
        in_specs=[pl.BlockSpec((1, tq, LANES), lambda bi, p, i: (bi, i, p)),
                  pl.BlockSpec((1, s, LANES), lambda bi, p, i: (bi, 0, pairs + p)),
                  pl.BlockSpec((1, s, LANES), lambda bi, p, i: (bi, 0, 2 * pairs + p)),
                  pl.BlockSpec((tk, 2 * tk), lambda bi, p, i: (0, 0))],
        out_specs=pl.BlockSpec((1, tq, LANES), lambda bi, p, i: (bi, i, p)),
        out_shape=jax.ShapeDtypeStruct((b, s, pairs * LANES), BF16),
        scratch_shapes=[pltpu.VMEM((2 * tq, LANES), F32), pltpu.VMEM((2 * tq, tk), F32)],
        compiler_params=_params(("arbitrary", "arbitrary", "arbitrary"), 32),
        name="sb_attention",
    )(qkv, qkv, qkv, jnp.asarray(u2, BF16))


HG_LEVELS = int(math.log2(HG_CHUNK))
HG_DIAG = HG_LEVELS


def _hgrn_tables():
    c = HG_CHUNK
    t = np.arange(c)[:, None]
    j = np.arange(c)[None, :]
    mats = [(j <= t), (j > t)]
    for lv in range(HG_LEVELS):
        m = 1 << lv
        mid = (t // (2 * m)) * (2 * m) + m - 1
        later = (t % (2 * m)) >= m
        mats.append(np.where(later, (j > mid) & (j <= t), (j > t) & (j <= mid)))
    stack = np.concatenate(mats, axis=0).astype(np.float32)
    s = j
    level = np.full((c, c), -1, np.int32)
    for lv in range(HG_LEVELS):
        level[(t > s) & ((t ^ s) >> lv == 1)] = lv
    level[t == s] = HG_DIAG
    return stack, level


def _hgrn_kernel(q_ref, f_ref, i_ref, g_ref, lbl_ref, gain_ref, gs_ref, lvl_ref, o_ref, st_ref, *, layer, nch):
    ch = HG_CHUNK

    @pl.when(pl.program_id(1) == 0)
    def _():
        st_ref[...] = jnp.zeros_like(st_ref)

    logits = lbl_ref[...]
    ex = jnp.exp(logits - jnp.max(logits, axis=0, keepdims=True))
    lb = jnp.sum(ex[:layer + 1], axis=0, keepdims=True) / jnp.sum(ex, axis=0, keepdims=True)
    gain = gain_ref[...]
    gs = gs_ref[...]
    lvl = lvl_ref[...]

    def chunk(ci, carry):
        r0 = pl.multiple_of(ci * ch, ch)
        f = f_ref[0, pl.ds(r0, ch), :]
        e = jnp.exp(-jnp.abs(f))
        inv = 1.0 / (1.0 + e)
        sig_f = jnp.where(f >= 0, inv, e * inv)
        sig_nf = jnp.where(f >= 0, e * inv, inv)
        log_f = jnp.log(lb + (1.0 - lb) * sig_f)
        key = (1.0 - lb) * sig_nf
        x = None
        for p in _split_bf16(log_f, 3):
            d = _dot(gs, p)
            x = d if x is None else x + d
        ex_all = jnp.exp(x)
        qa = _silu(q_ref[0, pl.ds(r0, ch), :])
        va = i_ref[0, pl.ds(r0, ch), :]
        ga = g_ref[0, pl.ds(r0, ch), :]
        for h in range(HG_HEADS):
            hs = slice(h * HG_DIM, (h + 1) * HG_DIM)
            qh, kh, vh = qa[:, hs], key[:, hs], va[:, hs]
            cum_e = ex_all[0:ch, hs]
            kdec_e = ex_all[ch:2 * ch, hs]
            st = st_ref[h]
            inter = _dot((qh * cum_e).astype(BF16), st.astype(BF16), NT_DIMS)
            scores = jnp.where(lvl == HG_DIAG, jnp.sum(qh * kh, axis=1, keepdims=True), 0.0)
            for lv in range(HG_LEVELS):
                p = ex_all[(2 + lv) * ch:(3 + lv) * ch, hs]
                s_lv = _dot((qh * p).astype(BF16), (kh * p).astype(BF16), NT_DIMS)
                scores = jnp.where(lvl == lv, s_lv, scores)
            vb = vh.astype(BF16)
            o = _dot(scores.astype(BF16), vb) + inter
            st_ref[h] = st * cum_e[ch - 1:ch, :] + _dot(vb, (kh * kdec_e).astype(BF16), TN_DIMS)
            o = _rms_scale(o) * gain[:, hs] * _silu(ga[:, hs])
            o_ref[0, pl.ds(r0, ch), hs] = o.astype(o_ref.dtype)
        return carry

    lax.fori_loop(0, nch, chunk, 0)


def _hgrn2(hg, lb_logits, gain, layer, tb=512):
    b, s, _ = hg.shape
    w = HG_HEADS * HG_DIM
    stack, level = _hgrn_tables()
    blk = lambda k: pl.BlockSpec((1, tb, w), lambda bi, i: (bi, i, k))
    return pl.pallas_call(
        functools.partial(_hgrn_kernel, layer=layer, nch=tb // HG_CHUNK),
        grid=(b, s // tb),
        in_specs=[blk(0), blk(1), blk(2), blk(3),
                  pl.BlockSpec(lb_logits.shape, lambda bi, i: (0, 0)),
                  pl.BlockSpec((1, w), lambda bi, i: (0, 0)),
                  pl.BlockSpec(stack.shape, lambda bi, i: (0, 0)),
                  pl.BlockSpec(level.shape, lambda bi, i: (0, 0))],
        out_specs=pl.BlockSpec((1, tb, w), lambda bi, i: (bi, i, 0)),
        out_shape=jax.ShapeDtypeStruct((b, s, w), BF16),
        scratch_shapes=[pltpu.VMEM((HG_HEADS, HG_DIM, HG_DIM), F32)],
        compiler_params=_params(("arbitrary", "arbitrary"), 40),
        name="hgrn2",
    )(hg, hg, hg, hg, lb_logits, gain, jnp.asarray(stack, BF16), jnp.asarray(level))


def _ffn_chunks(dff, width=512):
    edges = list(range(0, dff, width)) + [dff]
    return tuple(zip(edges[:-1], edges[1:]))


def _ffn_residual(x1, mod_ref, gain_ref, wg_ref, wu_ref, wo_ref, a_ref):
    h = _norm_mod(x1, gain_ref[...], mod_ref[0, 4:5, :], mod_ref[0, 3:4, :]).astype(BF16)
    for c0, c1 in _ffn_chunks(wg_ref.shape[1]):
        g = _dot(h, wg_ref[:, c0:c1])
        u = _dot(h, wu_ref[:, c0:c1])
        a_ref[:, c0:c1] = (_silu(g) * u).astype(BF16)
    return x1 + mod_ref[0, 5:6, :] * _dot(a_ref[...], wo_ref[...])


def _hybrid_out_ffn_kernel(x_ref, oa_ref, ob_ref, mod_ref, gain_ref, wa_ref, wb_ref, wg_ref, wu_ref, wo_ref,
                           o_ref, a_ref):
    mix = _dot(oa_ref[0], wa_ref[...]) + _dot(ob_ref[0], wb_ref[...])
    x1 = x_ref[0] + mod_ref[0, 2:3, :] * mix
    o_ref[0] = _ffn_residual(x1, mod_ref, gain_ref, wg_ref, wu_ref, wo_ref, a_ref)


def _hybrid_out_ffn(x, o_a, o_b, mod, gain_f, w_a, w_b, wg, wu, wo, tm=512):
    b, s, d = x.shape
    dff = wg.shape[1]
    row = lambda n: pl.BlockSpec((1, tm, n), lambda bi, i: (bi, i, 0))
    const = lambda shape: _resident(shape, lambda bi, i: (0, 0))
    return pl.pallas_call(
        _hybrid_out_ffn_kernel,
        grid=(b, s // tm),
        in_specs=[row(d), row(o_a.shape[2]), row(o_b.shape[2]),
                  pl.BlockSpec((1, N_MOD, d), lambda bi, i: (bi, 0, 0)),
                  pl.BlockSpec((1, d), lambda bi, i: (0, 0)),
                  const(w_a.shape), const(w_b.shape), const(wg.shape), const(wu.shape), const(wo.shape)],
        out_specs=row(d),
        out_shape=jax.ShapeDtypeStruct((b, s, d), F32),
        scratch_shapes=[pltpu.VMEM((tm, dff), BF16)],
        compiler_params=_params(("arbitrary", "arbitrary"), 56),
        name="hybrid_out_ffn",
    )(x, o_a, o_b, mod, gain_f, w_a, w_b, wg, wu, wo)


def _s5_glu_ffn_kernel(x_ref, y_ref, mod_ref, gain_ref, fin_ref, wv_ref, wt_ref, wg_ref, wu_ref, wo_ref,
                       o_ref, a_ref):
    y = y_ref[...].astype(BF16)
    mix = _dot(y, wv_ref[...]) * _sigmoid(_dot(y, wt_ref[...]))
    x1 = x_ref[0] + mod_ref[0, 2:3, :] * mix
    x2 = _ffn_residual(x1, mod_ref, gain_ref, wg_ref, wu_ref, wo_ref, a_ref)
    o_ref[0] = _rms_scale(x2) * fin_ref[...]


def _s5_glu_ffn(x, y_tm, mod, gain_f, final_g, wv, wt, wg, wu, wo, tm=512):
    b, s, d = x.shape
    dff = wg.shape[1]
    const = lambda shape: _resident(shape, lambda bi, i: (0, 0))
    return pl.pallas_call(
        _s5_glu_ffn_kernel,
        grid=(b, s // tm),
        in_specs=[pl.BlockSpec((1, tm, d), lambda bi, i: (bi, i, 0)),
                  pl.BlockSpec((tm, d), lambda bi, i: (i, bi)),
                  pl.BlockSpec((1, N_MOD, d), lambda bi, i: (bi, 0, 0)),
                  pl.BlockSpec((1, d), lambda bi, i: (0, 0)),
                  pl.BlockSpec((1, d), lambda bi, i: (0, 0)),
                  const(wv.shape), const(wt.shape), const(wg.shape), const(wu.shape), const(wo.shape)],
        out_specs=pl.BlockSpec((1, tm, d), lambda bi, i: (bi, i, 0)),
        out_shape=jax.ShapeDtypeStruct((b, s, d), F32),
        scratch_shapes=[pltpu.VMEM((tm, dff), BF16)],
        compiler_params=_params(("arbitrary", "arbitrary"), 56),
        name="s5_glu_ffn",
    )(x, y_tm, mod, gain_f, final_g, wv, wt, wg, wu, wo)


def _s5_discretise_kernel(lr_ref, li_ref, ldt_ref, br_ref, bi_ref, are_ref, aim_ref, bbr_ref, bbi_ref):
    lr = jnp.minimum(lr_ref[...], -1e-4)
    li = li_ref[...]
    dt = jnp.exp(ldt_ref[...])
    mag = jnp.exp(lr * dt)
    a_re = mag * jnp.cos(li * dt)
    a_im = mag * jnp.sin(li * dt)
    e_re = a_re - 1.0
    den = lr * lr + li * li
    z_re = (e_re * lr + a_im * li) / den
    z_im = (a_im * lr - e_re * li) / den
    are_ref[...] = a_re
    aim_ref[...] = a_im
    br = br_ref[...]
    bi = bi_ref[...]
    bbr_ref[...] = z_re[:, None, :] * br - z_im[:, None, :] * bi
    bbi_ref[...] = z_re[:, None, :] * bi + z_im[:, None, :] * br


def _s5_discretise(lam_re, lam_im, log_dt, b_re_t, b_im_t):
    g, p = lam_re.shape
    return pl.pallas_call(
        _s5_discretise_kernel,
        out_shape=[jax.ShapeDtypeStruct((g, p), F32), jax.ShapeDtypeStruct((g, p), F32),
                   jax.ShapeDtypeStruct(b_re_t.shape, F32), jax.ShapeDtypeStruct(b_re_t.shape, F32)],
        name="s5_discretise",
    )(lam_re, lam_im, log_dt.reshape(g, 1), b_re_t, b_im_t)


def _gelu_tanh(x):
    return 0.5 * x * (1.0 + jnp.tanh(math.sqrt(2.0 / math.pi) * (x + 0.044715 * (x * x * x))))


def _s5_scan_kernel(u_ref, bm_ref, cr_ref, ci_ref, are_ref, aim_ref, d_ref, o_ref,
                    bu_ref, xr_ref, xi_ref, st_ref, *, ts, nb, half):
    @pl.when(pl.program_id(1) == 0)
    def _():
        st_ref[...] = jnp.zeros_like(st_ref)

    u = u_ref[...].reshape(ts * nb, LANES)
    bu_ref[...] = _dot(u.astype(BF16), bm_ref[0])
    a_re = jnp.broadcast_to(are_ref[0], (nb, half))
    a_im = jnp.broadcast_to(aim_ref[0], (nb, half))

    def step(t, state):
        x_re, x_im = state
        r0 = pl.multiple_of(t * nb, nb)
        n_re = a_re * x_re - a_im * x_im + bu_ref[pl.ds(r0, nb), 0:half]
        n_im = a_re * x_im + a_im * x_re + bu_ref[pl.ds(r0, nb), half:2 * half]
        xr_ref[pl.ds(r0, nb), :] = n_re
        xi_ref[pl.ds(r0, nb), :] = n_im
        return n_re, n_im

    x_re, x_im = lax.fori_loop(0, ts, step, (st_ref[:, 0:half], st_ref[:, half:2 * half]), unroll=8)
    st_ref[:, 0:half] = x_re
    st_ref[:, half:2 * half] = x_im
    y = _dot(xr_ref[...].astype(BF16), cr_ref[0]) - _dot(xi_ref[...].astype(BF16), ci_ref[0])
    y = _gelu_tanh(y + d_ref[0] * u)
    o_ref[...] = y.reshape(o_ref.shape)


def _s5_scan(u_tm, nb, bmat, cr, ci, a_re, a_im, d_skip, ts=128):
    s = u_tm.shape[0]
    d = u_tm.shape[1] // nb
    slabs = d // LANES
    half = bmat.shape[2] // 2
    u3 = u_tm.reshape(s, nb, d)
    io = pl.BlockSpec((ts, nb, LANES), lambda o, i: (i, 0, o))
    per_slab = lambda shape: pl.BlockSpec((1,) + shape, lambda o, i: (o, 0, 0))
    y = pl.pallas_call(
        functools.partial(_s5_scan_kernel, ts=ts, nb=nb, half=half),
        grid=(slabs, s // ts),
        in_specs=[io, per_slab(bmat.shape[1:]), per_slab(cr.shape[1:]), per_slab(ci.shape[1:]),
                  per_slab((1, half)), per_slab((1, half)), per_slab((1, LANES))],
        out_specs=io,
        out_shape=jax.ShapeDtypeStruct((s, nb, d), F32),
        scratch_shapes=[pltpu.VMEM((ts * nb, 2 * half), F32), pltpu.VMEM((ts * nb, half), F32),
                        pltpu.VMEM((ts * nb, half), F32), pltpu.VMEM((nb, 2 * half), F32)],
        compiler_params=_params(("arbitrary", "arbitrary"), 40),
        name="s5_scan",
    )(u3, bmat, cr, ci, a_re, a_im, d_skip)
    return y.reshape(s, nb * d)


def _block_diag_slabs(t, per_slab):
    g, a, b = t.shape
    t = t.reshape(g // per_slab, per_slab, a, b)
    eye = jnp.eye(per_slab, dtype=t.dtype)
    return jnp.einsum('ogab,gk->ogakb', t, eye).reshape(g // per_slab, per_slab * a, per_slab * b)


def kernel(x, c, norm_mix_g, norm_ffn_g, ada_w, ada_b, ffn_w_in, ffn_w_out, final_norm_g, hy_w_in, hy_w_out, hg_norm_g, hg_lb_logits, s5_w_in, s5_lam_re, s5_lam_im, s5_log_dt, s5_b_re, s5_b_im, s5_c_re, s5_c_im, s5_d, s5_w_glu):
    b, s, d = x.shape
    depth = ada_w.shape[0]
    dff = ffn_w_out.shape[1]
    sb_w = SB_HEADS * SB_HEAD_DIM
    mod = _adaln_mod(c, ada_w, ada_b).reshape(depth, b, N_MOD, d)
    per_slab = LANES // S5_GROUP

    for layer in range(depth):
        i = layer // 2
        wg = ffn_w_in[layer, :, :dff].astype(BF16)
        wu = ffn_w_in[layer, :, dff:].astype(BF16)
        wo = ffn_w_out[layer].astype(BF16)
        gain_m = norm_mix_g[layer].reshape(1, d)
        gain_f = norm_ffn_g[layer].reshape(1, d)
        if layer % 2 == 0:
            qkv, hg = _hybrid_in_proj(x, mod[layer], gain_m, hy_w_in[i].astype(BF16))
            o_a = _sb_attention(qkv)
            o_b = _hgrn2(hg, hg_lb_logits, hg_norm_g[i].reshape(1, -1), layer)
            w_out = hy_w_out[i].astype(BF16)
            x = _hybrid_out_ffn(x, o_a, o_b, mod[layer], gain_f, w_out[:sb_w], w_out[sb_w:], wg, wu, wo)
        else:
            u_tm = _s5_in_proj(x, mod[layer], gain_m, s5_w_in[i].astype(BF16))
            a_re, a_im, bb_re, bb_im = _s5_discretise(
                s5_lam_re[i], s5_lam_im[i], s5_log_dt[i],
                jnp.swapaxes(s5_b_re[i], 1, 2), jnp.swapaxes(s5_b_im[i], 1, 2))
            slabs = d // LANES
            bmat = jnp.concatenate([_block_diag_slabs(bb_re, per_slab), _block_diag_slabs(bb_im, per_slab)],
                                   axis=2).astype(BF16)
            cr = _block_diag_slabs(jnp.swapaxes(s5_c_re[i], 1, 2), per_slab).astype(BF16)
            ci = _block_diag_slabs(jnp.swapaxes(s5_c_im[i], 1, 2), per_slab).astype(BF16)
            half = per_slab * S5_STATE
            y_tm = _s5_scan(u_tm, b, bmat, cr, ci, a_re.reshape(slabs, 1, half), a_im.reshape(slabs, 1, half),
                            s5_d[i].reshape(slabs, 1, LANES))
            w_glu = s5_w_glu[i].astype(BF16)
            last = layer == depth - 1
            x = _s5_glu_ffn(x, y_tm, mod[layer], gain_f, final_norm_g.reshape(1, d),
                            w_glu[:, :d], w_glu[:, d:], wg, wu, wo)
            assert last, "final norm is fused into the S5 layer's FFN kernel"
    return x
```

```python
import functools
import math

import numpy as np
import jax
import jax.numpy as jnp
from jax import lax
from jax.experimental import pallas as pl
from jax.experimental.pallas import tpu as pltpu

F32 = jnp.float32
BF16 = jnp.bfloat16

EPS = 1e-6
N_MOD = 6
SB_HEADS = 8
SB_HEAD_DIM = 64
HG_HEADS = 4
HG_DIM = 128
HG_CHUNK = 64
S5_GROUP = 16
S5_STATE = 64

LANES = 128
SUBLANES = 8
MIB = 1 << 20

SB_DEAD_LOG_WEIGHT = 88.0

NT_DIMS = (((1,), (1,)), ((), ()))
TN_DIMS = (((0,), (0,)), ((), ()))


def _dot(a, b, dims=None):
    if dims is None:
        return jnp.dot(a, b, preferred_element_type=F32)
    return lax.dot_general(a, b, dims, preferred_element_type=F32)


def _split_bf16(x, n):
    parts = []
    rest = x
    for i in range(n):
        p = rest.astype(BF16)
        parts.append(p)
        if i + 1 < n:
            rest = rest - p.astype(F32)
    return parts


def _sigmoid(x):
    return 1.0 / (1.0 + jnp.exp(-x))


def _silu(x):
    return x * _sigmoid(x)


def _rms_scale(x):
    return x * lax.rsqrt(jnp.mean(x * x, axis=-1, keepdims=True) + EPS)


def _norm_mod(x, gain, scale, shift):
    return _rms_scale(x) * (gain * (1.0 + scale)) + shift


def _params(semantics, vmem_mib):
    return pltpu.CompilerParams(dimension_semantics=semantics, vmem_limit_bytes=vmem_mib * MIB)


def _resident(shape, index_map):
    return pl.BlockSpec(shape, index_map, pipeline_mode=pl.Buffered(1))


def _mod_kernel(c_ref, w_ref, b_ref, o_ref):
    a_hi, a_lo = _split_bf16(_silu(c_ref[...]), 2)
    w_hi, w_lo = _split_bf16(w_ref[0], 2)
    acc = _dot(a_hi, w_hi) + _dot(a_hi, w_lo) + _dot(a_lo, w_hi)
    o_ref[0] = acc + b_ref[0]


def _adaln_mod(c, ada_w, ada_b, tn=1536):
    depth, d, n = ada_w.shape
    b = c.shape[0]
    return pl.pallas_call(
        _mod_kernel,
        grid=(depth, n // tn),
        in_specs=[pl.BlockSpec((b, d), lambda l, j: (0, 0)),
                  pl.BlockSpec((1, d, tn), lambda l, j: (l, 0, j)),
                  pl.BlockSpec((1, 1, tn), lambda l, j: (l, 0, j))],
        out_specs=pl.BlockSpec((1, b, tn), lambda l, j: (l, 0, j)),
        out_shape=jax.ShapeDtypeStruct((depth, b, n), F32),
        compiler_params=_params(("arbitrary", "arbitrary"), 40),
        name="adaln_mod",
    )(c, ada_w, ada_b.reshape(depth, 1, n))


def _normmod_proj_kernel(x_ref, mod_ref, gain_ref, w_ref, *o_refs, splits):
    x = x_ref[0]
    h = _norm_mod(x, gain_ref[...], mod_ref[0, 1:2, :], mod_ref[0, 0:1, :]).astype(BF16)
    for o_ref, (c0, c1) in zip(o_refs, splits):
        y = _dot(h, w_ref[:, c0:c1])
        o_ref[...] = y.reshape(o_ref.shape).astype(o_ref.dtype)


def _hybrid_in_proj(x, mod, gain, w, tm=512):
    b, s, d = x.shape
    n = w.shape[1]
    n_a = 3 * SB_HEADS * SB_HEAD_DIM
    return pl.pallas_call(
        functools.partial(_normmod_proj_kernel, splits=((0, n_a), (n_a, n))),
        grid=(b, s // tm),
        in_specs=[pl.BlockSpec((1, tm, d), lambda bi, i: (bi, i, 0)),
                  pl.BlockSpec((1, N_MOD, d), lambda bi, i: (bi, 0, 0)),
                  pl.BlockSpec((1, d), lambda bi, i: (0, 0)),
                  _resident((d, n), lambda bi, i: (0, 0))],
        out_specs=[pl.BlockSpec((1, tm, n_a), lambda bi, i: (bi, i, 0)),
                   pl.BlockSpec((1, tm, n - n_a), lambda bi, i: (bi, i, 0))],
        out_shape=[jax.ShapeDtypeStruct((b, s, n_a), BF16),
                   jax.ShapeDtypeStruct((b, s, n - n_a), F32)],
        compiler_params=_params(("arbitrary", "arbitrary"), 48),
        name="hybrid_in_proj",
    )(x, mod, gain, w)


def _s5_in_proj(x, mod, gain, w, tm=512):
    b, s, d = x.shape
    return pl.pallas_call(
        functools.partial(_normmod_proj_kernel, splits=((0, d),)),
        grid=(b, s // tm),
        in_specs=[pl.BlockSpec((1, tm, d), lambda bi, i: (bi, i, 0)),
                  pl.BlockSpec((1, N_MOD, d), lambda bi, i: (bi, 0, 0)),
                  pl.BlockSpec((1, d), lambda bi, i: (0, 0)),
                  _resident((d, d), lambda bi, i: (0, 0))],
        out_specs=[pl.BlockSpec((tm, d), lambda bi, i: (i, bi))],
        out_shape=[jax.ShapeDtypeStruct((s, b * d), F32)],
        compiler_params=_params(("arbitrary", "arbitrary"), 40),
        name="s5_in_proj",
    )(x, mod, gain, w)[0]


SB_BAND = 2 * LANES


def _sb_band(qq, k_ref, v_ref, lanes, off, mask, carry, u2):
    kj = k_ref[0, pl.ds(off, SB_BAND), lanes]
    vj = v_ref[0, pl.ds(off, SB_BAND), lanes]
    z = _dot(qq, kj, NT_DIMS)
    log_beta = jnp.minimum(z, 0.0) - jnp.log(1.0 + jnp.exp(-jnp.abs(z)))
    log_keep = jnp.where(mask, log_beta - z, 0.0)
    hi, lo = _split_bf16(log_keep, 2)
    sums = _dot(hi, u2) + _dot(lo, u2)
    tail = sums[:, :SB_BAND]
    if carry is not None:
        tail = tail + jnp.concatenate([carry] * (SB_BAND // LANES), axis=1)
    w = jnp.where(mask, jnp.exp(log_beta + tail), 0.0)
    row_sum = sums[:, SB_BAND:]
    return _dot(w.astype(BF16), vj), row_sum if carry is None else carry + row_sum


def _sb_attn_kernel(q_ref, k_ref, v_ref, u2_ref, o_ref, acc_ref, carry_ref, *, tq, dh, scale, pairs):
    q0 = pl.program_id(1) * tq
    u2 = u2_ref[...]
    lane = lax.broadcasted_iota(jnp.int32, (tq, LANES), 1)
    row = lax.broadcasted_iota(jnp.int32, (2 * tq, SB_BAND), 0)
    col = lax.broadcasted_iota(jnp.int32, (2 * tq, SB_BAND), 1)
    rel = jnp.where(row >= tq, row - tq, row) - col
    lanes = [slice(p * LANES, (p + 1) * LANES) for p in range(pairs)]

    qqs = []
    for p in range(pairs):
        qs = q_ref[0, :, lanes[p]] * jnp.asarray(scale, q_ref.dtype)
        zero = jnp.zeros_like(qs)
        qqs.append(jnp.concatenate([jnp.where(lane < dh, qs, zero), jnp.where(lane >= dh, qs, zero)], axis=0))

    off0 = pl.multiple_of(jnp.maximum(q0 + tq - SB_BAND, 0), LANES)
    causal = rel > (off0 - q0)
    carry_max = None
    for p in range(pairs):
        pv, carry = _sb_band(qqs[p], k_ref, v_ref, lanes[p], off0, causal, None, u2)
        acc_ref[p] = pv
        carry_ref[p] = carry
        carry_max = carry if carry_max is None else jnp.maximum(carry_max, carry)

    def live(state):
        off_prev, cmax = state
        return jnp.logical_and(off_prev > 0, cmax > -SB_DEAD_LOG_WEIGHT)

    def earlier_band(state):
        off_prev, _ = state
        off = pl.multiple_of(jnp.maximum(off_prev - SB_BAND, 0), LANES)
        fresh = col < (off_prev - off)
        cmax = None
        for p in range(pairs):
            pv, carry = _sb_band(qqs[p], k_ref, v_ref, lanes[p], off, fresh, carry_ref[p], u2)
            acc_ref[p] += pv
            carry_ref[p] = carry
            cmax = carry if cmax is None else jnp.maximum(cmax, carry)
        return off, jnp.max(cmax)

    lax.while_loop(live, earlier_band, (off0, jnp.max(carry_max)))
    for p in range(pairs):
        acc = acc_ref[p]
        o_ref[0, :, lanes[p]] = jnp.where(lane < dh, acc[:tq], acc[tq:]).astype(o_ref.dtype)


def _sb_attention(qkv, tq=LANES):
    b, s, _ = qkv.shape
    width = SB_HEADS * SB_HEAD_DIM
    pairs = width // LANES
    j = np.arange(SB_BAND)
    u2 = np.concatenate([(j[:, None] > j[None, :]).astype(np.float32), np.ones((SB_BAND, LANES), np.float32)],
                        axis=1)
    return pl.pallas_call(
        functools.partial(_sb_attn_kernel, tq=tq, dh=SB_HEAD_DIM, scale=SB_HEAD_DIM ** -0.5, pairs=pairs),
        grid=(b, s // tq),
        in_specs=[pl.BlockSpec((1, tq, width), lambda bi, i: (bi, i, 0)),
                  pl.BlockSpec((1, s, width), lambda bi, i: (bi, 0, 1)),
                  pl.BlockSpec((1, s, width), lambda bi, i: (bi, 0, 2)),
                  pl.BlockSpec(u2.shape, lambda bi, i: (0, 0))],
        out_specs=pl.BlockSpec((1, tq, width), lambda bi, i: (bi, i, 0)),
        out_shape=jax.ShapeDtypeStruct((b, s, width), BF16),
        scratch_shapes=[pltpu.VMEM((pairs, 2 * tq, LANES), F32), pltpu.VMEM((pairs, 2 * tq, LANES), F32)],
        compiler_params=_params(("arbitrary", "arbitrary"), 40),
        name="sb_attention",
    )(qkv, qkv, qkv, jnp.asarray(u2, BF16))


HG_LEVELS = int(math.log2(HG_CHUNK))
HG_DIAG = HG_LEVELS


def _hgrn_tables():
    c = HG_CHUNK
    t = np.arange(c)[:, None]
    j = np.arange(c)[None, :]
    mats = [(j <= t), (j > t)]
    for lv in range(HG_LEVELS):
        m = 1 << lv
        mid = (t // (2 * m)) * (2 * m) + m - 1
        later = (t % (2 * m)) >= m
        mats.append(np.where(later, (j > mid) & (j <= t), (j > t) & (j <= mid)))
    stack = np.concatenate(mats, axis=0).astype(np.float32)
    s = j
    level = np.full((c, c), -1, np.int32)
    for lv in range(HG_LEVELS):
        level[(t > s) & ((t ^ s) >> lv == 1)] = lv
    level[t == s] = HG_DIAG
    return stack, level


def _hgrn_kernel(q_ref, f_ref, i_ref, g_ref, lbl_ref, gain_ref, gs_ref, lvl_ref, o_ref, st_ref, *, layer, nch):
    ch = HG_CHUNK

    @pl.when(pl.program_id(1) == 0)
    def _():
        st_ref[...] = jnp.zeros_like(st_ref)

    logits = lbl_ref[...]
    ex = jnp.exp(logits - jnp.max(logits, axis=0, keepdims=True))
    lb = jnp.sum(ex[:layer + 1], axis=0, keepdims=True) / jnp.sum(ex, axis=0, keepdims=True)
    gain = gain_ref[...]
    gs = gs_ref[...]
    lvl = lvl_ref[...]

    def chunk(ci, carry):
        r0 = pl.multiple_of(ci * ch, ch)
        f = f_ref[0, pl.ds(r0, ch), :]
        e = jnp.exp(-jnp.abs(f))
        inv = 1.0 / (1.0 + e)
        sig_f = jnp.where(f >= 0, inv, e * inv)
        sig_nf = jnp.where(f >= 0, e * inv, inv)
        log_f = jnp.log(lb + (1.0 - lb) * sig_f)
        key = (1.0 - lb) * sig_nf
        x = None
        for p in _split_bf16(log_f, 2):
            d = _dot(gs, p)
            x = d if x is None else x + d
        ex_all = jnp.exp(x)
        qa = _silu(q_ref[0, pl.ds(r0, ch), :])
        va = i_ref[0, pl.ds(r0, ch), :]
        ga = g_ref[0, pl.ds(r0, ch), :]
        for h in range(HG_HEADS):
            hs = slice(h * HG_DIM, (h + 1) * HG_DIM)
            qh, kh, vh = qa[:, hs], key[:, hs], va[:, hs]
            cum_e = ex_all[0:ch, hs]
            kdec_e = ex_all[ch:2 * ch, hs]
            st = st_ref[h]
            inter = _dot((qh * cum_e).astype(BF16), st.astype(BF16), NT_DIMS)
            scores = jnp.where(lvl == HG_DIAG, jnp.sum(qh * kh, axis=1, keepdims=True), 0.0)
            for lv in range(HG_LEVELS):
                p = ex_all[(2 + lv) * ch:(3 + lv) * ch, hs]
                s_lv = _dot((qh * p).astype(BF16), (kh * p).astype(BF16), NT_DIMS)
                scores = jnp.where(lvl == lv, s_lv, scores)
            vb = vh.astype(BF16)
            o = _dot(scores.astype(BF16), vb) + inter
            st_ref[h] = st * cum_e[ch - 1:ch, :] + _dot(vb, (kh * kdec_e).astype(BF16), TN_DIMS)
            o = _rms_scale(o) * gain[:, hs] * _silu(ga[:, hs])
            o_ref[0, pl.ds(r0, ch), hs] = o.astype(o_ref.dtype)
        return carry

    lax.fori_loop(0, nch, chunk, 0, unroll=2)


def _hgrn2(hg, lb_logits, gain, layer, tb=512):
    b, s, _ = hg.shape
    w = HG_HEADS * HG_DIM
    stack, level = _hgrn_tables()
    blk = lambda k: pl.BlockSpec((1, tb, w), lambda bi, i: (bi, i, k))
    return pl.pallas_call(
        functools.partial(_hgrn_kernel, layer=layer, nch=tb // HG_CHUNK),
        grid=(b, s // tb),
        in_specs=[blk(0), blk(1), blk(2), blk(3),
                  pl.BlockSpec(lb_logits.shape, lambda bi, i: (0, 0)),
                  pl.BlockSpec((1, w), lambda bi, i: (0, 0)),
                  pl.BlockSpec(stack.shape, lambda bi, i: (0, 0)),
                  pl.BlockSpec(level.shape, lambda bi, i: (0, 0))],
        out_specs=pl.BlockSpec((1, tb, w), lambda bi, i: (bi, i, 0)),
        out_shape=jax.ShapeDtypeStruct((b, s, w), BF16),
        scratch_shapes=[pltpu.VMEM((HG_HEADS, HG_DIM, HG_DIM), F32)],
        compiler_params=_params(("arbitrary", "arbitrary"), 40),
        name="hgrn2",
    )(hg, hg, hg, hg, lb_logits, gain, jnp.asarray(stack, BF16), jnp.asarray(level))


def _ffn_chunks(dff, width=512):
    edges = list(range(0, dff, width)) + [dff]
    return tuple(zip(edges[:-1], edges[1:]))


def _ffn_residual(x1, mod_ref, gain_ref, wg_ref, wu_ref, wo_ref, a_ref):
    h = _norm_mod(x1, gain_ref[...], mod_ref[0, 4:5, :], mod_ref[0, 3:4, :]).astype(BF16)
    for c0, c1 in _ffn_chunks(wg_ref.shape[1]):
        g = _dot(h, wg_ref[:, c0:c1])
        u = _dot(h, wu_ref[:, c0:c1])
        a_ref[:, c0:c1] = (_silu(g) * u).astype(BF16)
    return x1 + mod_ref[0, 5:6, :] * _dot(a_ref[...], wo_ref[...])


def _hybrid_out_ffn_kernel(x_ref, oa_ref, ob_ref, mod_ref, gain_ref, wa_ref, wb_ref, wg_ref, wu_ref, wo_ref,
                           o_ref, a_ref):
    mix = _dot(oa_ref[0], wa_ref[...]) + _dot(ob_ref[0], wb_ref[...])
    x1 = x_ref[0] + mod_ref[0, 2:3, :] * mix
    o_ref[0] = _ffn_residual(x1, mod_ref, gain_ref, wg_ref, wu_ref, wo_ref, a_ref)


def _hybrid_out_ffn(x, o_a, o_b, mod, gain_f, w_a, w_b, wg, wu, wo, tm=512):
    b, s, d = x.shape
    dff = wg.shape[1]
    row = lambda n: pl.BlockSpec((1, tm, n), lambda bi, i: (bi, i, 0))
    const = lambda shape: _resident(shape, lambda bi, i: (0, 0))
    return pl.pallas_call(
        _hybrid_out_ffn_kernel,
        grid=(b, s // tm),
        in_specs=[row(d), row(o_a.shape[2]), row(o_b.shape[2]),
                  pl.BlockSpec((1, N_MOD, d), lambda bi, i: (bi, 0, 0)),
                  pl.BlockSpec((1, d), lambda bi, i: (0, 0)),
                  const(w_a.shape), const(w_b.shape), const(wg.shape), const(wu.shape), const(wo.shape)],
        out_specs=row(d),
        out_shape=jax.ShapeDtypeStruct((b, s, d), F32),
        scratch_shapes=[pltpu.VMEM((tm, dff), BF16)],
        compiler_params=_params(("arbitrary", "arbitrary"), 56),
        name="hybrid_out_ffn",
    )(x, o_a, o_b, mod, gain_f, w_a, w_b, wg, wu, wo)


def _s5_glu_ffn_kernel(x_ref, y_ref, mod_ref, gain_ref, fin_ref, wv_ref, wt_ref, wg_ref, wu_ref, wo_ref,
                       o_ref, a_ref):
    y = y_ref[...].astype(BF16)
    mix = _dot(y, wv_ref[...]) * _sigmoid(_dot(y, wt_ref[...]))
    x1 = x_ref[0] + mod_ref[0, 2:3, :] * mix
    x2 = _ffn_residual(x1, mod_ref, gain_ref, wg_ref, wu_ref, wo_ref, a_ref)
    o_ref[0] = _rms_scale(x2) * fin_ref[...]


def _s5_glu_ffn(x, y_tm, mod, gain_f, final_g, wv, wt, wg, wu, wo, tm=512):
    b, s, d = x.shape
    dff = wg.shape[1]
    const = lambda shape: _resident(shape, lambda bi, i: (0, 0))
    return pl.pallas_call(
        _s5_glu_ffn_kernel,
        grid=(b, s // tm),
        in_specs=[pl.BlockSpec((1, tm, d), lambda bi, i: (bi, i, 0)),
                  pl.BlockSpec((tm, d), lambda bi, i: (i, bi)),
                  pl.BlockSpec((1, N_MOD, d), lambda bi, i: (bi, 0, 0)),
                  pl.BlockSpec((1, d), lambda bi, i: (0, 0)),
                  pl.BlockSpec((1, d), lambda bi, i: (0, 0)),
                  const(wv.shape), const(wt.shape), const(wg.shape), const(wu.shape), const(wo.shape)],
        out_specs=pl.BlockSpec((1, tm, d), lambda bi, i: (bi, i, 0)),
        out_shape=jax.ShapeDtypeStruct((b, s, d), F32),
        scratch_shapes=[pltpu.VMEM((tm, dff), BF16)],
        compiler_params=_params(("arbitrary", "arbitrary"), 56),
        name="s5_glu_ffn",
    )(x, y_tm, mod, gain_f, final_g, wv, wt, wg, wu, wo)


def _s5_discretise_kernel(lr_ref, li_ref, ldt_ref, br_ref, bi_ref, are_ref, aim_ref, bbr_ref, bbi_ref):
    lr = jnp.minimum(lr_ref[...], -1e-4)
    li = li_ref[...]
    dt = jnp.exp(ldt_ref[...])
    mag = jnp.exp(lr * dt)
    a_re = mag * jnp.cos(li * dt)
    a_im = mag * jnp.sin(li * dt)
    e_re = a_re - 1.0
    den = lr * lr + li * li
    z_re = (e_re * lr + a_im * li) / den
    z_im = (a_im * lr - e_re * li) / den
    are_ref[...] = a_re
    aim_ref[...] = a_im
    br = br_ref[...]
    bi = bi_ref[...]
    bbr_ref[...] = z_re[:, None, :] * br - z_im[:, None, :] * bi
    bbi_ref[...] = z_re[:, None, :] * bi + z_im[:, None, :] * br


def _s5_discretise(lam_re, lam_im, log_dt, b_re_t, b_im_t):
    g, p = lam_re.shape
    return pl.pallas_call(
        _s5_discretise_kernel,
        out_shape=[jax.ShapeDtypeStruct((g, p), F32), jax.ShapeDtypeStruct((g, p), F32),
                   jax.ShapeDtypeStruct(b_re_t.shape, F32), jax.ShapeDtypeStruct(b_re_t.shape, F32)],
        name="s5_discretise",
    )(lam_re, lam_im, log_dt.reshape(g, 1), b_re_t, b_im_t)


def _gelu_tanh(x):
    return 0.5 * x * (1.0 + jnp.tanh(math.sqrt(2.0 / math.pi) * (x + 0.044715 * (x * x * x))))


def _s5_scan_kernel(u_ref, bm_ref, cr_ref, ci_ref, are_ref, aim_ref, d_ref, o_ref,
                    bu_ref, xr_ref, xi_ref, st_ref, *, ts, nb, half):
    @pl.when(pl.program_id(1) == 0)
    def _():
        st_ref[...] = jnp.zeros_like(st_ref)

    u = u_ref[...].reshape(ts * nb, LANES)
    bu_ref[...] = _dot(u.astype(BF16), bm_ref[0])
    a_re = jnp.broadcast_to(are_ref[0], (nb, half))
    a_im = jnp.broadcast_to(aim_ref[0], (nb, half))

    def step(t, state):
        x_re, x_im = state
        r0 = pl.multiple_of(t * nb, nb)
        n_re = a_re * x_re - a_im * x_im + bu_ref[pl.ds(r0, nb), 0:half]
        n_im = a_re * x_im + a_im * x_re + bu_ref[pl.ds(r0, nb), half:2 * half]
        xr_ref[pl.ds(r0, nb), :] = n_re
        xi_ref[pl.ds(r0, nb), :] = n_im
        return n_re, n_im

    x_re, x_im = lax.fori_loop(0, ts, step, (st_ref[:, 0:half], st_ref[:, half:2 * half]), unroll=8)
    st_ref[:, 0:half] = x_re
    st_ref[:, half:2 * half] = x_im
    y = _dot(xr_ref[...].astype(BF16), cr_ref[0]) - _dot(xi_ref[...].astype(BF16), ci_ref[0])
    y = _gelu_tanh(y + d_ref[0] * u)
    o_ref[...] = y.reshape(o_ref.shape)


def _s5_scan(u_tm, nb, bmat, cr, ci, a_re, a_im, d_skip, ts=128):
    s = u_tm.shape[0]
    d = u_tm.shape[1] // nb
    slabs = d // LANES
    half = bmat.shape[2] // 2
    u3 = u_tm.reshape(s, nb, d)
    io = pl.BlockSpec((ts, nb, LANES), lambda o, i: (i, 0, o))
    per_slab = lambda shape: pl.BlockSpec((1,) + shape, lambda o, i: (o, 0, 0))
    y = pl.pallas_call(
        functools.partial(_s5_scan_kernel, ts=ts, nb=nb, half=half),
        grid=(slabs, s // ts),
        in_specs=[io, per_slab(bmat.shape[1:]), per_slab(cr.shape[1:]), per_slab(ci.shape[1:]),
                  per_slab((1, half)), per_slab((1, half)), per_slab((1, LANES))],
        out_specs=io,
        out_shape=jax.ShapeDtypeStruct((s, nb, d), F32),
        scratch_shapes=[pltpu.VMEM((ts * nb, 2 * half), F32), pltpu.VMEM((ts * nb, half), F32),
                        pltpu.VMEM((ts * nb, half), F32), pltpu.VMEM((nb, 2 * half), F32)],
        compiler_params=_params(("arbitrary", "arbitrary"), 40),
        name="s5_scan",
    )(u3, bmat, cr, ci, a_re, a_im, d_skip)
    return y.reshape(s, nb * d)


def _block_diag_slabs(t, per_slab):
    g, a, b = t.shape
    t = t.reshape(g // per_slab, per_slab, a, b)
    eye = jnp.eye(per_slab, dtype=t.dtype)
    return jnp.einsum('ogab,gk->ogakb', t, eye).reshape(g // per_slab, per_slab * a, per_slab * b)


def kernel(x, c, norm_mix_g, norm_ffn_g, ada_w, ada_b, ffn_w_in, ffn_w_out, final_norm_g, hy_w_in, hy_w_out, hg_norm_g, hg_lb_logits, s5_w_in, s5_lam_re, s5_lam_im, s5_log_dt, s5_b_re, s5_b_im, s5_c_re, s5_c_im, s5_d, s5_w_glu):
    b, s, d = x.shape
    depth = ada_w.shape[0]
    dff = ffn_w_out.shape[1]
    sb_w = SB_HEADS * SB_HEAD_DIM
    mod = _adaln_mod(c, ada_w, ada_b).reshape(depth, b, N_MOD, d)
    per_slab = LANES // S5_GROUP

    for layer in range(depth):
        i = layer // 2
        wg = ffn_w_in[layer, :, :dff].astype(BF16)
        wu = ffn_w_in[layer, :, dff:].astype(BF16)
        wo = ffn_w_out[layer].astype(BF16)
        gain_m = norm_mix_g[layer].reshape(1, d)
        gain_f = norm_ffn_g[layer].reshape(1, d)
        if layer % 2 == 0:
            qkv, hg = _hybrid_in_proj(x, mod[layer], gain_m, hy_w_in[i].astype(BF16))
            o_a = _sb_attention(qkv)
            o_b = _hgrn2(hg, hg_lb_logits, hg_norm_g[i].reshape(1, -1), layer)
            w_out = hy_w_out[i].astype(BF16)
            x = _hybrid_out_ffn(x, o_a, o_b, mod[layer], gain_f, w_out[:sb_w], w_out[sb_w:], wg, wu, wo)
        else:
            u_tm = _s5_in_proj(x, mod[layer], gain_m, s5_w_in[i].astype(BF16))
            a_re, a_im, bb_re, bb_im = _s5_discretise(
                s5_lam_re[i], s5_lam_im[i], s5_log_dt[i],
                jnp.swapaxes(s5_b_re[i], 1, 2), jnp.swapaxes(s5_b_im[i], 1, 2))
            slabs = d // LANES
            bmat = jnp.concatenate([_block_diag_slabs(bb_re, per_slab), _block_diag_slabs(bb_im, per_slab)],
                                   axis=2).astype(BF16)
            cr = _block_diag_slabs(jnp.swapaxes(s5_c_re[i], 1, 2), per_slab).astype(BF16)
            ci = _block_diag_slabs(jnp.swapaxes(s5_c_im[i], 1, 2), per_slab).astype(BF16)
            half = per_slab * S5_STATE
            y_tm = _s5_scan(u_tm, b, bmat, cr, ci, a_re.reshape(slabs, 1, half), a_im.reshape(slabs, 1, half),
                            s5_d[i].reshape(slabs, 1, LANES))
            w_glu = s5_w_glu[i].astype(BF16)
            last = layer == depth - 1
            x = _s5_glu_ffn(x, y_tm, mod[layer], gain_f, final_norm_g.reshape(1, d),
                            w_glu[:, :d], w_glu[:, d:], wg, wu, wo)
            assert last, "final norm is fused into the S5 layer's FFN kernel"
    return x
```

```python
import functools
import math

import numpy as np
import jax
import jax.numpy as jnp
from jax import lax
from jax.experimental import pallas as pl
from jax.experimental.pallas import tpu as pltpu

F32 = jnp.float32
BF16 = jnp.bfloat16

EPS = 1e-6
N_MOD = 6
SB_HEADS = 8
SB_HEAD_DIM = 64
HG_HEADS = 4
HG_DIM = 128
HG_CHUNK = 64
S5_GROUP = 16
S5_STATE = 64

LANES = 128
SUBLANES = 8
MIB = 1 << 20

SB_DEAD_LOG_WEIGHT = 88.0

NT_DIMS = (((1,), (1,)), ((), ()))
TN_DIMS = (((0,), (0,)), ((), ()))


def _dot(a, b, dims=None):
    if dims is None:
        return jnp.dot(a, b, preferred_element_type=F32)
    return lax.dot_general(a, b, dims, preferred_element_type=F32)


def _split_bf16(x, n):
    parts = []
    rest = x
    for i in range(n):
        p = rest.astype(BF16)
        parts.append(p)
        if i + 1 < n:
            rest = rest - p.astype(F32)
    return parts


def _sigmoid(x):
    return 1.0 / (1.0 + jnp.exp(-x))


def _silu(x):
    return x * _sigmoid(x)


def _rms_scale(x):
    return x * lax.rsqrt(jnp.mean(x * x, axis=-1, keepdims=True) + EPS)


def _norm_mod(x, gain, scale, shift):
    return _rms_scale(x) * (gain * (1.0 + scale)) + shift


def _params(semantics, vmem_mib):
    return pltpu.CompilerParams(dimension_semantics=semantics, vmem_limit_bytes=vmem_mib * MIB)


def _resident(shape, index_map):
    return pl.BlockSpec(shape, index_map, pipeline_mode=pl.Buffered(1))


def _mod_kernel(c_ref, w_ref, b_ref, o_ref):
    a_hi, a_lo = _split_bf16(_silu(c_ref[...]), 2)
    w_hi, w_lo = _split_bf16(w_ref[0], 2)
    acc = _dot(a_hi, w_hi) + _dot(a_hi, w_lo) + _dot(a_lo, w_hi)
    o_ref[0] = acc + b_ref[0]


def _adaln_mod(c, ada_w, ada_b, tn=1536):
    depth, d, n = ada_w.shape
    b = c.shape[0]
    return pl.pallas_call(
        _mod_kernel,
        grid=(depth, n // tn),
        in_specs=[pl.BlockSpec((b, d), lambda l, j: (0, 0)),
                  pl.BlockSpec((1, d, tn), lambda l, j: (l, 0, j)),
                  pl.BlockSpec((1, 1, tn), lambda l, j: (l, 0, j))],
        out_specs=pl.BlockSpec((1, b, tn), lambda l, j: (l, 0, j)),
        out_shape=jax.ShapeDtypeStruct((depth, b, n), F32),
        compiler_params=_params(("arbitrary", "arbitrary"), 40),
        name="adaln_mod",
    )(c, ada_w, ada_b.reshape(depth, 1, n))


def _normmod_proj_kernel(x_ref, mod_ref, gain_ref, w_ref, *o_refs, splits):
    x = x_ref[0]
    h = _norm_mod(x, gain_ref[...], mod_ref[0, 1:2, :], mod_ref[0, 0:1, :]).astype(BF16)
    for o_ref, (c0, c1) in zip(o_refs, splits):
        y = _dot(h, w_ref[:, c0:c1])
        o_ref[...] = y.reshape(o_ref.shape).astype(o_ref.dtype)


def _hybrid_in_proj(x, mod, gain, w, tm=512):
    b, s, d = x.shape
    n = w.shape[1]
    n_a = 3 * SB_HEADS * SB_HEAD_DIM
    return pl.pallas_call(
        functools.partial(_normmod_proj_kernel, splits=((0, n_a), (n_a, n))),
        grid=(b, s // tm),
        in_specs=[pl.BlockSpec((1, tm, d), lambda bi, i: (bi, i, 0)),
                  pl.BlockSpec((1, N_MOD, d), lambda bi, i: (bi, 0, 0)),
                  pl.BlockSpec((1, d), lambda bi, i: (0, 0)),
                  _resident((d, n), lambda bi, i: (0, 0))],
        out_specs=[pl.BlockSpec((1, tm, n_a), lambda bi, i: (bi, i, 0)),
                   pl.BlockSpec((1, tm, n - n_a), lambda bi, i: (bi, i, 0))],
        out_shape=[jax.ShapeDtypeStruct((b, s, n_a), BF16),
                   jax.ShapeDtypeStruct((b, s, n - n_a), F32)],
        compiler_params=_params(("arbitrary", "arbitrary"), 48),
        name="hybrid_in_proj",
    )(x, mod, gain, w)


def _s5_in_proj(x, mod, gain, w, tm=512):
    b, s, d = x.shape
    return pl.pallas_call(
        functools.partial(_normmod_proj_kernel, splits=((0, d),)),
        grid=(b, s // tm),
        in_specs=[pl.BlockSpec((1, tm, d), lambda bi, i: (bi, i, 0)),
                  pl.BlockSpec((1, N_MOD, d), lambda bi, i: (bi, 0, 0)),
                  pl.BlockSpec((1, d), lambda bi, i: (0, 0)),
                  _resident((d, d), lambda bi, i: (0, 0))],
        out_specs=[pl.BlockSpec((1, tm, d), lambda bi, i: (bi, i, 0))],
        out_shape=[jax.ShapeDtypeStruct((b, s, d), F32)],
        compiler_params=_params(("arbitrary", "arbitrary"), 40),
        name="s5_in_proj",
    )(x, mod, gain, w)[0]


SB_BAND = 2 * LANES
SB_TAIL_ROWS = LANES


def _sb_band(qq, kj, vj, mask, carry, u2):
    z = _dot(qq, kj, NT_DIMS)
    log_beta = jnp.minimum(z, 0.0) - jnp.log(1.0 + jnp.exp(-jnp.abs(z)))
    log_keep = jnp.where(mask, log_beta - z, 0.0)
    hi, lo = _split_bf16(log_keep, 2)
    sums = _dot(hi, u2) + _dot(lo, u2)
    tail = sums[:, :SB_BAND]
    if carry is not None:
        tail = tail + jnp.concatenate([carry] * (SB_BAND // LANES), axis=1)
    w = jnp.where(mask, jnp.exp(log_beta + tail), 0.0)
    row_sum = sums[:, SB_BAND:]
    return _dot(w.astype(BF16), vj), row_sum if carry is None else carry + row_sum


def _sb_attn_kernel(q_ref, k_ref, v_ref, u2_ref, ur_ref, o_ref, acc_ref, carry_ref, tacc_ref, tcarry_ref,
                    *, tq, dh, scale, pairs):
    q0 = pl.program_id(1) * tq
    rows = 2 * tq
    cap = SB_TAIL_ROWS
    u2 = u2_ref[...]
    lane = lax.broadcasted_iota(jnp.int32, (tq, LANES), 1)
    row = lax.broadcasted_iota(jnp.int32, (rows, SB_BAND), 0)
    col = lax.broadcasted_iota(jnp.int32, (rows, SB_BAND), 1)
    rel = jnp.where(row >= tq, row - tq, row) - col
    lanes = [slice(p * LANES, (p + 1) * LANES) for p in range(pairs)]

    qqs = []
    for p in range(pairs):
        qs = q_ref[0, :, lanes[p]] * jnp.asarray(scale, q_ref.dtype)
        zero = jnp.zeros_like(qs)
        qqs.append(jnp.concatenate([jnp.where(lane < dh, qs, zero), jnp.where(lane >= dh, qs, zero)], axis=0))

    off0 = pl.multiple_of(jnp.maximum(q0 + tq - SB_BAND, 0), LANES)
    causal = rel > (off0 - q0)
    carry_max = None
    for p in range(pairs):
        kj = k_ref[0, pl.ds(off0, SB_BAND), lanes[p]]
        vj = v_ref[0, pl.ds(off0, SB_BAND), lanes[p]]
        pv, carry = _sb_band(qqs[p], kj, vj, causal, None, u2)
        acc_ref[p] = pv
        carry_ref[p] = carry
        carry_max = carry if carry_max is None else jnp.maximum(carry_max, carry)
    cmax0 = jnp.max(carry_max)

    def live(state):
        off_prev, cmax = state
        return jnp.logical_and(off_prev > 0, cmax > -SB_DEAD_LOG_WEIGHT)

    def earlier_offset(off_prev):
        return pl.multiple_of(jnp.maximum(off_prev - SB_BAND, 0), LANES)

    def walk_all_rows():
        def earlier_band(state):
            off_prev, _ = state
            off = earlier_offset(off_prev)
            fresh = col < (off_prev - off)
            cmax = None
            for p in range(pairs):
                kj = k_ref[0, pl.ds(off, SB_BAND), lanes[p]]
                vj = v_ref[0, pl.ds(off, SB_BAND), lanes[p]]
                pv, carry = _sb_band(qqs[p], kj, vj, fresh, carry_ref[p], u2)
                acc_ref[p] += pv
                carry_ref[p] = carry
                cmax = carry if cmax is None else jnp.maximum(cmax, carry)
            return off, jnp.max(cmax)

        lax.while_loop(live, earlier_band, (off0, cmax0))

    def walk_live_rows(live_t, counts, total):
        slot = lax.broadcasted_iota(jnp.int32, (cap, rows), 0).astype(F32)
        base = jnp.zeros((SUBLANES, LANES), F32)
        sels = []
        for p in range(pairs):
            rank = _dot(live_t[p].astype(BF16), ur_ref[...]) + jnp.concatenate([base] * (rows // LANES), axis=1)
            hit = jnp.logical_and(rank[0:1, :] == slot, live_t[p][0:1, :] > 0.5)
            sels.append(jnp.where(hit, 1.0, 0.0).astype(BF16))
            base = base + counts[p]
        q_c = jnp.concatenate([_dot(sels[p], qqs[p]) for p in range(pairs)], axis=1).astype(BF16)
        c_c = None
        for p in range(pairs):
            for part in _split_bf16(carry_ref[p], 3):
                d = _dot(sels[p], part)
                c_c = d if c_c is None else c_c + d
        used = lax.broadcasted_iota(jnp.int32, (cap, LANES), 0).astype(F32) < total[0:1, :]
        tcarry_ref[...] = jnp.where(used, c_c, -1e30)
        tacc_ref[...] = jnp.zeros_like(tacc_ref)
        col_c = lax.broadcasted_iota(jnp.int32, (cap, SB_BAND), 1)

        def earlier_band(state):
            off_prev, _ = state
            off = earlier_offset(off_prev)
            fresh = col_c < (off_prev - off)
            kj = k_ref[0, pl.ds(off, SB_BAND), :]
            vj = v_ref[0, pl.ds(off, SB_BAND), :]
            pv, carry = _sb_band(q_c, kj, vj, fresh, tcarry_ref[...], u2)
            tacc_ref[...] += pv
            tcarry_ref[...] = carry
            return off, jnp.max(carry)

        lax.while_loop(live, earlier_band, (off0, cmax0))
        for p in range(pairs):
            add = None
            for part in _split_bf16(tacc_ref[:, lanes[p]], 2):
                d = _dot(sels[p], part, TN_DIMS)
                add = d if add is None else add + d
            acc_ref[p] += add

    @pl.when(jnp.logical_and(off0 > 0, cmax0 > -SB_DEAD_LOG_WEIGHT))
    def _():
        lane0 = jnp.where(lax.broadcasted_iota(jnp.int32, (SUBLANES, LANES), 1) == 0, 1.0, 0.0).astype(BF16)
        ones = jnp.ones((rows, LANES), BF16)
        live_t, counts, total = [], [], None
        for p in range(pairs):
            alive = jnp.where(carry_ref[p] > -SB_DEAD_LOG_WEIGHT, 1.0, 0.0).astype(BF16)
            lt = _dot(lane0, alive, NT_DIMS)
            cnt = _dot(lt.astype(BF16), ones)
            live_t.append(lt)
            counts.append(cnt)
            total = cnt if total is None else total + cnt
        n_live = jnp.max(total)

        @pl.when(n_live <= cap)
        def _():
            walk_live_rows(live_t, counts, total)

        @pl.when(n_live > cap)
        def _():
            walk_all_rows()

    for p in range(pairs):
        acc = acc_ref[p]
        o_ref[0, :, lanes[p]] = jnp.where(lane < dh, acc[:tq], acc[tq:]).astype(o_ref.dtype)


def _sb_attention(qkv, tq=LANES):
    b, s, _ = qkv.shape
    width = SB_HEADS * SB_HEAD_DIM
    pairs = width // LANES
    j = np.arange(SB_BAND)
    u2 = np.concatenate([(j[:, None] > j[None, :]).astype(np.float32), np.ones((SB_BAND, LANES), np.float32)],
                        axis=1)
    r = np.arange(2 * tq)
    ur = (r[:, None] < r[None, :]).astype(np.float32)
    return pl.pallas_call(
        functools.partial(_sb_attn_kernel, tq=tq, dh=SB_HEAD_DIM, scale=SB_HEAD_DIM ** -0.5, pairs=pairs),
        grid=(b, s // tq),
        in_specs=[pl.BlockSpec((1, tq, width), lambda bi, i: (bi, i, 0)),
                  pl.BlockSpec((1, s, width), lambda bi, i: (bi, 0, 1)),
                  pl.BlockSpec((1, s, width), lambda bi, i: (bi, 0, 2)),
                  pl.BlockSpec(u2.shape, lambda bi, i: (0, 0)),
                  pl.BlockSpec(ur.shape, lambda bi, i: (0, 0))],
        out_specs=pl.BlockSpec((1, tq, width), lambda bi, i: (bi, i, 0)),
        out_shape=jax.ShapeDtypeStruct((b, s, width), BF16),
        scratch_shapes=[pltpu.VMEM((pairs, 2 * tq, LANES), F32), pltpu.VMEM((pairs, 2 * tq, LANES), F32),
                        pltpu.VMEM((SB_TAIL_ROWS, width), F32), pltpu.VMEM((SB_TAIL_ROWS, LANES), F32)],
        compiler_params=_params(("arbitrary", "arbitrary"), 40),
        name="sb_attention",
    )(qkv, qkv, qkv, jnp.asarray(u2, BF16), jnp.asarray(ur, BF16))


HG_LEVELS = int(math.log2(HG_CHUNK))
HG_DIAG = HG_LEVELS


def _hgrn_tables():
    c = HG_CHUNK
    t = np.arange(c)[:, None]
    j = np.arange(c)[None, :]
    mats = [(j <= t), (j > t)]
    for lv in range(HG_LEVELS):
        m = 1 << lv
        mid = (t // (2 * m)) * (2 * m) + m - 1
        later = (t % (2 * m)) >= m
        mats.append(np.where(later, (j > mid) & (j <= t), (j > t) & (j <= mid)))
    stack = np.concatenate(mats, axis=0).astype(np.float32)
    s = j
    level = np.full((c, c), -1, np.int32)
    for lv in range(HG_LEVELS):
        level[(t > s) & ((t ^ s) >> lv == 1)] = lv
    level[t == s] = HG_DIAG
    return stack, level


def _hgrn_kernel(q_ref, f_ref, i_ref, g_ref, lbl_ref, gain_ref, gs_ref, lvl_ref, o_ref,
                 st_ref, ex_ref, sc_ref, kv_ref, acc_ref, *, layer, nch):
    ch = HG_CHUNK

    @pl.when(pl.program_id(1) == 0)
    def _():
        st_ref[...] = jnp.zeros_like(st_ref)

    logits = lbl_ref[...]
    ex = jnp.exp(logits - jnp.max(logits, axis=0, keepdims=True))
    lb = jnp.sum(ex[:layer + 1], axis=0, keepdims=True) / jnp.sum(ex, axis=0, keepdims=True)
    gs = gs_ref[...]
    lvl = lvl_ref[...]
    tiles = [(c, h, slice(c * ch, (c + 1) * ch), slice(h * HG_DIM, (h + 1) * HG_DIM))
             for c in range(nch) for h in range(HG_HEADS)]

    f = f_ref[0]
    e = jnp.exp(-jnp.abs(f))
    inv = 1.0 / (1.0 + e)
    sig_f = jnp.where(f >= 0, inv, e * inv)
    sig_nf = jnp.where(f >= 0, e * inv, inv)
    log_hi, log_lo = _split_bf16(jnp.log(lb + (1.0 - lb) * sig_f), 2)
    key = (1.0 - lb) * sig_nf
    for c in range(nch):
        rows = slice(c * ch, (c + 1) * ch)
        ex_c = jnp.exp(_dot(gs, log_hi[rows]) + _dot(gs, log_lo[rows]))
        for g in range(2 + HG_LEVELS):
            ex_ref[g, rows, :] = ex_c[g * ch:(g + 1) * ch]
    qa = _silu(q_ref[0])
    vb = i_ref[0].astype(BF16)
    cum_e = ex_ref[0]
    qc = (qa * cum_e).astype(BF16)
    kd = (key * ex_ref[1]).astype(BF16)
    qk = qa * key
    for c, h, rows, hs in tiles:
        sc_ref[c, h] = jnp.where(lvl == HG_DIAG, jnp.sum(qk[rows, hs], axis=1, keepdims=True), 0.0)
        kv_ref[c, h] = _dot(vb[rows, hs], kd[rows, hs], TN_DIMS)
    for lv in range(HG_LEVELS):
        p = ex_ref[2 + lv]
        qe = (qa * p).astype(BF16)
        ke = (key * p).astype(BF16)
        for c, h, rows, hs in tiles:
            s_lv = _dot(qe[rows, hs], ke[rows, hs], NT_DIMS)
            sc_ref[c, h] = jnp.where(lvl == lv, s_lv, sc_ref[c, h])
    for c, h, rows, hs in tiles:
        acc_ref[rows, hs] = _dot(sc_ref[c, h].astype(BF16), vb[rows, hs])
    for c, h, rows, hs in tiles:
        st = st_ref[h]
        kv_c = kv_ref[c, h]
        kv_ref[c, h] = st
        st_ref[h] = st * cum_e[(c + 1) * ch - 1:(c + 1) * ch, hs] + kv_c
    for c, h, rows, hs in tiles:
        acc_ref[rows, hs] += _dot(qc[rows, hs], kv_ref[c, h].astype(BF16), NT_DIMS)
    gate = gain_ref[...] * _silu(g_ref[0])
    for h in range(HG_HEADS):
        hs = slice(h * HG_DIM, (h + 1) * HG_DIM)
        o_ref[0, :, hs] = (_rms_scale(acc_ref[:, hs]) * gate[:, hs]).astype(o_ref.dtype)


def _hgrn2(hg, lb_logits, gain, layer, tb=512):
    b, s, _ = hg.shape
    w = HG_HEADS * HG_DIM
    stack, level = _hgrn_tables()
    blk = lambda k: pl.BlockSpec((1, tb, w), lambda bi, i: (bi, i, k))
    nch = tb // HG_CHUNK
    return pl.pallas_call(
        functools.partial(_hgrn_kernel, layer=layer, nch=nch),
        grid=(b, s // tb),
        in_specs=[blk(0), blk(1), blk(2), blk(3),
                  pl.BlockSpec(lb_logits.shape, lambda bi, i: (0, 0)),
                  pl.BlockSpec((1, w), lambda bi, i: (0, 0)),
                  pl.BlockSpec(stack.shape, lambda bi, i: (0, 0)),
                  pl.BlockSpec(level.shape, lambda bi, i: (0, 0))],
        out_specs=pl.BlockSpec((1, tb, w), lambda bi, i: (bi, i, 0)),
        out_shape=jax.ShapeDtypeStruct((b, s, w), BF16),
        scratch_shapes=[pltpu.VMEM((HG_HEADS, HG_DIM, HG_DIM), F32),
                        pltpu.VMEM((2 + HG_LEVELS, tb, w), F32),
                        pltpu.VMEM((nch, HG_HEADS, HG_CHUNK, HG_CHUNK), F32),
                        pltpu.VMEM((nch, HG_HEADS, HG_DIM, HG_DIM), F32),
                        pltpu.VMEM((tb, w), F32)],
        compiler_params=_params(("arbitrary", "arbitrary"), 40),
        name="hgrn2",
    )(hg, hg, hg, hg, lb_logits, gain, jnp.asarray(stack, BF16), jnp.asarray(level))


def _ffn_chunks(dff, width=512):
    edges = list(range(0, dff, width)) + [dff]
    return tuple(zip(edges[:-1], edges[1:]))


def _ffn_residual(x1, mod_ref, gain_ref, wg_ref, wu_ref, wo_ref, a_ref):
    h = _norm_mod(x1, gain_ref[...], mod_ref[0, 4:5, :], mod_ref[0, 3:4, :]).astype(BF16)
    for c0, c1 in _ffn_chunks(wg_ref.shape[1]):
        g = _dot(h, wg_ref[:, c0:c1])
        u = _dot(h, wu_ref[:, c0:c1])
        a_ref[:, c0:c1] = (_silu(g) * u).astype(BF16)
    return x1 + mod_ref[0, 5:6, :] * _dot(a_ref[...], wo_ref[...])


def _hybrid_out_ffn_kernel(x_ref, oa_ref, ob_ref, mod_ref, gain_ref, wa_ref, wb_ref, wg_ref, wu_ref, wo_ref,
                           o_ref, a_ref):
    mix = _dot(oa_ref[0], wa_ref[...]) + _dot(ob_ref[0], wb_ref[...])
    x1 = x_ref[0] + mod_ref[0, 2:3, :] * mix
    o_ref[0] = _ffn_residual(x1, mod_ref, gain_ref, wg_ref, wu_ref, wo_ref, a_ref)


def _hybrid_out_ffn(x, o_a, o_b, mod, gain_f, w_a, w_b, wg, wu, wo, tm=512):
    b, s, d = x.shape
    dff = wg.shape[1]
    row = lambda n: pl.BlockSpec((1, tm, n), lambda bi, i: (bi, i, 0))
    const = lambda shape: _resident(shape, lambda bi, i: (0, 0))
    return pl.pallas_call(
        _hybrid_out_ffn_kernel,
        grid=(b, s // tm),
        in_specs=[row(d), row(o_a.shape[2]), row(o_b.shape[2]),
                  pl.BlockSpec((1, N_MOD, d), lambda bi, i: (bi, 0, 0)),
                  pl.BlockSpec((1, d), lambda bi, i: (0, 0)),
                  const(w_a.shape), const(w_b.shape), const(wg.shape), const(wu.shape), const(wo.shape)],
        out_specs=row(d),
        out_shape=jax.ShapeDtypeStruct((b, s, d), F32),
        scratch_shapes=[pltpu.VMEM((tm, dff), BF16)],
        compiler_params=_params(("arbitrary", "arbitrary"), 56),
        name="hybrid_out_ffn",
    )(x, o_a, o_b, mod, gain_f, w_a, w_b, wg, wu, wo)


def _s5_glu_ffn_kernel(x_ref, y_ref, mod_ref, gain_ref, fin_ref, wv_ref, wt_ref, wg_ref, wu_ref, wo_ref,
                       o_ref, a_ref):
    y = y_ref[0]
    mix = _dot(y, wv_ref[...]) * _sigmoid(_dot(y, wt_ref[...]))
    x1 = x_ref[0] + mod_ref[0, 2:3, :] * mix
    x2 = _ffn_residual(x1, mod_ref, gain_ref, wg_ref, wu_ref, wo_ref, a_ref)
    o_ref[0] = _rms_scale(x2) * fin_ref[...]


def _s5_glu_ffn(x, y, mod, gain_f, final_g, wv, wt, wg, wu, wo, tm=512):
    b, s, d = x.shape
    dff = wg.shape[1]
    const = lambda shape: _resident(shape, lambda bi, i: (0, 0))
    return pl.pallas_call(
        _s5_glu_ffn_kernel,
        grid=(b, s // tm),
        in_specs=[pl.BlockSpec((1, tm, d), lambda bi, i: (bi, i, 0)),
                  pl.BlockSpec((1, tm, d), lambda bi, i: (bi, i, 0)),
                  pl.BlockSpec((1, N_MOD, d), lambda bi, i: (bi, 0, 0)),
                  pl.BlockSpec((1, d), lambda bi, i: (0, 0)),
                  pl.BlockSpec((1, d), lambda bi, i: (0, 0)),
                  const(wv.shape), const(wt.shape), const(wg.shape), const(wu.shape), const(wo.shape)],
        out_specs=pl.BlockSpec((1, tm, d), lambda bi, i: (bi, i, 0)),
        out_shape=jax.ShapeDtypeStruct((b, s, d), F32),
        scratch_shapes=[pltpu.VMEM((tm, dff), BF16)],
        compiler_params=_params(("arbitrary", "arbitrary"), 56),
        name="s5_glu_ffn",
    )(x, y, mod, gain_f, final_g, wv, wt, wg, wu, wo)


def _s5_discretise_kernel(lr_ref, li_ref, ldt_ref, br_ref, bi_ref, are_ref, aim_ref, bbr_ref, bbi_ref):
    lr = jnp.minimum(lr_ref[...], -1e-4)
    li = li_ref[...]
    dt = jnp.exp(ldt_ref[...])
    mag = jnp.exp(lr * dt)
    a_re = mag * jnp.cos(li * dt)
    a_im = mag * jnp.sin(li * dt)
    e_re = a_re - 1.0
    den = lr * lr + li * li
    z_re = (e_re * lr + a_im * li) / den
    z_im = (a_im * lr - e_re * li) / den
    are_ref[...] = a_re
    aim_ref[...] = a_im
    br = br_ref[...]
    bi = bi_ref[...]
    bbr_ref[...] = z_re[:, None, :] * br - z_im[:, None, :] * bi
    bbi_ref[...] = z_re[:, None, :] * bi + z_im[:, None, :] * br


def _s5_discretise(lam_re, lam_im, log_dt, b_re_t, b_im_t):
    g, p = lam_re.shape
    return pl.pallas_call(
        _s5_discretise_kernel,
        out_shape=[jax.ShapeDtypeStruct((g, p), F32), jax.ShapeDtypeStruct((g, p), F32),
                   jax.ShapeDtypeStruct(b_re_t.shape, F32), jax.ShapeDtypeStruct(b_re_t.shape, F32)],
        name="s5_discretise",
    )(lam_re, lam_im, log_dt.reshape(g, 1), b_re_t, b_im_t)


def _gelu_tanh(x):
    return 0.5 * x * (1.0 + jnp.tanh(math.sqrt(2.0 / math.pi) * (x + 0.044715 * (x * x * x))))


def _s5_scan_kernel(u_ref, bm_ref, cr_ref, ci_ref, are_ref, aim_ref, d_ref, o_ref,
                    bu_ref, xr_ref, xi_ref, st_ref, *, ts, nb, half):
    @pl.when(pl.program_id(1) == 0)
    def _():
        st_ref[...] = jnp.zeros_like(st_ref)

    u = jnp.swapaxes(u_ref[...], 0, 1).reshape(ts * nb, LANES)
    bu_ref[...] = _dot(u.astype(BF16), bm_ref[0])
    a_re = jnp.broadcast_to(are_ref[0], (nb, half))
    a_im = jnp.broadcast_to(aim_ref[0], (nb, half))

    def step(t, state):
        x_re, x_im = state
        r0 = pl.multiple_of(t * nb, nb)
        n_re = a_re * x_re - a_im * x_im + bu_ref[pl.ds(r0, nb), 0:half]
        n_im = a_re * x_im + a_im * x_re + bu_ref[pl.ds(r0, nb), half:2 * half]
        xr_ref[pl.ds(r0, nb), :] = n_re
        xi_ref[pl.ds(r0, nb), :] = n_im
        return n_re, n_im

    x_re, x_im = lax.fori_loop(0, ts, step, (st_ref[:, 0:half], st_ref[:, half:2 * half]), unroll=8)
    st_ref[:, 0:half] = x_re
    st_ref[:, half:2 * half] = x_im
    y = _dot(xr_ref[...].astype(BF16), cr_ref[0]) - _dot(xi_ref[...].astype(BF16), ci_ref[0])
    y = _gelu_tanh(y + d_ref[0] * u)
    o_ref[...] = jnp.swapaxes(y.reshape(ts, nb, LANES), 0, 1).astype(o_ref.dtype)


def _s5_scan(u, bmat, cr, ci, a_re, a_im, d_skip, ts=128):
    nb, s, d = u.shape
    slabs = d // LANES
    half = bmat.shape[2] // 2
    io = pl.BlockSpec((nb, ts, LANES), lambda o, i: (0, i, o))
    per_slab = lambda shape: pl.BlockSpec((1,) + shape, lambda o, i: (o, 0, 0))
    return pl.pallas_call(
        functools.partial(_s5_scan_kernel, ts=ts, nb=nb, half=half),
        grid=(slabs, s // ts),
        in_specs=[io, per_slab(bmat.shape[1:]), per_slab(cr.shape[1:]), per_slab(ci.shape[1:]),
                  per_slab((1, half)), per_slab((1, half)), per_slab((1, LANES))],
        out_specs=io,
        out_shape=jax.ShapeDtypeStruct((nb, s, d), BF16),
        scratch_shapes=[pltpu.VMEM((ts * nb, 2 * half), F32), pltpu.VMEM((ts * nb, half), F32),
                        pltpu.VMEM((ts * nb, half), F32), pltpu.VMEM((nb, 2 * half), F32)],
        compiler_params=_params(("arbitrary", "arbitrary"), 40),
        name="s5_scan",
    )(u, bmat, cr, ci, a_re, a_im, d_skip)


def _block_diag_slabs(t, per_slab):
    g, a, b = t.shape
    t = t.reshape(g // per_slab, per_slab, a, b)
    eye = jnp.eye(per_slab, dtype=t.dtype)
    return jnp.einsum('ogab,gk->ogakb', t, eye).reshape(g // per_slab, per_slab * a, per_slab * b)


def kernel(x, c, norm_mix_g, norm_ffn_g, ada_w, ada_b, ffn_w_in, ffn_w_out, final_norm_g, hy_w_in, hy_w_out, hg_norm_g, hg_lb_logits, s5_w_in, s5_lam_re, s5_lam_im, s5_log_dt, s5_b_re, s5_b_im, s5_c_re, s5_c_im, s5_d, s5_w_glu):
    b, s, d = x.shape
    depth = ada_w.shape[0]
    dff = ffn_w_out.shape[1]
    sb_w = SB_HEADS * SB_HEAD_DIM
    mod = _adaln_mod(c, ada_w, ada_b).reshape(depth, b, N_MOD, d)
    per_slab = LANES // S5_GROUP

    for layer in range(depth):
        i = layer // 2
        wg = ffn_w_in[layer, :, :dff].astype(BF16)
        wu = ffn_w_in[layer, :, dff:].astype(BF16)
        wo = ffn_w_out[layer].astype(BF16)
        gain_m = norm_mix_g[layer].reshape(1, d)
        gain_f = norm_ffn_g[layer].reshape(1, d)
        if layer % 2 == 0:
            qkv, hg = _hybrid_in_proj(x, mod[layer], gain_m, hy_w_in[i].astype(BF16))
            o_a = _sb_attention(qkv)
            o_b = _hgrn2(hg, hg_lb_logits, hg_norm_g[i].reshape(1, -1), layer)
            w_out = hy_w_out[i].astype(BF16)
            x = _hybrid_out_ffn(x, o_a, o_b, mod[layer], gain_f, w_out[:sb_w], w_out[sb_w:], wg, wu, wo)
        else:
            u = _s5_in_proj(x, mod[layer], gain_m, s5_w_in[i].astype(BF16))
            a_re, a_im, bb_re, bb_im = _s5_discretise(
                s5_lam_re[i], s5_lam_im[i], s5_log_dt[i],
                jnp.swapaxes(s5_b_re[i], 1, 2), jnp.swapaxes(s5_b_im[i], 1, 2))
            slabs = d // LANES
            bmat = jnp.concatenate([_block_diag_slabs(bb_re, per_slab), _block_diag_slabs(bb_im, per_slab)],
                                   axis=2).astype(BF16)
            cr = _block_diag_slabs(jnp.swapaxes(s5_c_re[i], 1, 2), per_slab).astype(BF16)
            ci = _block_diag_slabs(jnp.swapaxes(s5_c_im[i], 1, 2), per_slab).astype(BF16)
            half = per_slab * S5_STATE
            y = _s5_scan(u, bmat, cr, ci, a_re.reshape(slabs, 1, half), a_im.reshape(slabs, 1, half),
                         s5_d[i].reshape(slabs, 1, LANES))
            w_glu = s5_w_glu[i].astype(BF16)
            last = layer == depth - 1
            x = _s5_glu_ffn(x, y, mod[layer], gain_f, final_norm_g.reshape(1, d),
                            w_glu[:, :d], w_glu[:, d:], wg, wu, wo)
            assert last, "final norm is fused into the S5 layer's FFN kernel"
    return x
```

```python
import functools
import math

import numpy as np
import jax
import jax.numpy as jnp
from jax import lax
from jax.experimental import pallas as pl
from jax.experimental.pallas import tpu as pltpu

F32 = jnp.float32
BF16 = jnp.bfloat16

EPS = 1e-6
N_MOD = 6
SB_HEADS = 8
SB_HEAD_DIM = 64
HG_HEADS = 4
HG_DIM = 128
HG_CHUNK = 64
S5_GROUP = 16
S5_STATE = 64

LANES = 128
SUBLANES = 8
MIB = 1 << 20

SB_DEAD_LOG_WEIGHT = 88.0

NT_DIMS = (((1,), (1,)), ((), ()))
TN_DIMS = (((0,), (0,)), ((), ()))


def _dot(a, b, dims=None):
    if dims is None:
        return jnp.dot(a, b, preferred_element_type=F32)
    return lax.dot_general(a, b, dims, preferred_element_type=F32)


def _split_bf16(x, n):
    parts = []
    rest = x
    for i in range(n):
        p = rest.astype(BF16)
        parts.append(p)
        if i + 1 < n:
            rest = rest - p.astype(F32)
    return parts


def _sigmoid(x):
    return 1.0 / (1.0 + jnp.exp(-x))


def _silu(x):
    return x * _sigmoid(x)


def _rms_scale(x):
    return x * lax.rsqrt(jnp.mean(x * x, axis=-1, keepdims=True) + EPS)


def _norm_mod(x, gain, scale, shift):
    return _rms_scale(x) * (gain * (1.0 + scale)) + shift


def _params(semantics, vmem_mib):
    return pltpu.CompilerParams(dimension_semantics=semantics, vmem_limit_bytes=vmem_mib * MIB)


def _resident(shape, index_map):
    return pl.BlockSpec(shape, index_map, pipeline_mode=pl.Buffered(1))


def _mod_kernel(c_ref, w_ref, b_ref, o_ref):
    a_hi, a_lo = _split_bf16(_silu(c_ref[...]), 2)
    w_hi, w_lo = _split_bf16(w_ref[0], 2)
    acc = _dot(a_hi, w_hi) + _dot(a_hi, w_lo) + _dot(a_lo, w_hi)
    o_ref[0] = acc + b_ref[0]


def _adaln_mod(c, ada_w, ada_b, tn=1536):
    depth, d, n = ada_w.shape
    b = c.shape[0]
    return pl.pallas_call(
        _mod_kernel,
        grid=(depth, n // tn),
        in_specs=[pl.BlockSpec((b, d), lambda l, j: (0, 0)),
                  pl.BlockSpec((1, d, tn), lambda l, j: (l, 0, j)),
                  pl.BlockSpec((1, 1, tn), lambda l, j: (l, 0, j))],
        out_specs=pl.BlockSpec((1, b, tn), lambda l, j: (l, 0, j)),
        out_shape=jax.ShapeDtypeStruct((depth, b, n), F32),
        compiler_params=_params(("arbitrary", "arbitrary"), 40),
        name="adaln_mod",
    )(c, ada_w, ada_b.reshape(depth, 1, n))


def _normmod_proj_kernel(x_ref, mod_ref, gain_ref, w_ref, *o_refs, splits):
    x = x_ref[0]
    h = _norm_mod(x, gain_ref[...], mod_ref[0, 1:2, :], mod_ref[0, 0:1, :]).astype(BF16)
    for o_ref, (c0, c1) in zip(o_refs, splits):
        y = _dot(h, w_ref[:, c0:c1])
        o_ref[...] = y.reshape(o_ref.shape).astype(o_ref.dtype)


def _hybrid_in_proj(x, mod, gain, w, tm=512):
    b, s, d = x.shape
    n = w.shape[1]
    n_a = 3 * SB_HEADS * SB_HEAD_DIM
    return pl.pallas_call(
        functools.partial(_normmod_proj_kernel, splits=((0, n_a), (n_a, n))),
        grid=(b, s // tm),
        in_specs=[pl.BlockSpec((1, tm, d), lambda bi, i: (bi, i, 0)),
                  pl.BlockSpec((1, N_MOD, d), lambda bi, i: (bi, 0, 0)),
                  pl.BlockSpec((1, d), lambda bi, i: (0, 0)),
                  _resident((d, n), lambda bi, i: (0, 0))],
        out_specs=[pl.BlockSpec((1, tm, n_a), lambda bi, i: (bi, i, 0)),
                   pl.BlockSpec((1, tm, n - n_a), lambda bi, i: (bi, i, 0))],
        out_shape=[jax.ShapeDtypeStruct((b, s, n_a), BF16),
                   jax.ShapeDtypeStruct((b, s, n - n_a), F32)],
        compiler_params=_params(("arbitrary", "arbitrary"), 48),
        name="hybrid_in_proj",
    )(x, mod, gain, w)


def _s5_in_proj(x, mod, gain, w, tm=512):
    b, s, d = x.shape
    return pl.pallas_call(
        functools.partial(_normmod_proj_kernel, splits=((0, d),)),
        grid=(b, s // tm),
        in_specs=[pl.BlockSpec((1, tm, d), lambda bi, i: (bi, i, 0)),
                  pl.BlockSpec((1, N_MOD, d), lambda bi, i: (bi, 0, 0)),
                  pl.BlockSpec((1, d), lambda bi, i: (0, 0)),
                  _resident((d, d), lambda bi, i: (0, 0))],
        out_specs=[pl.BlockSpec((1, tm, d), lambda bi, i: (bi, i, 0))],
        out_shape=[jax.ShapeDtypeStruct((b, s, d), F32)],
        compiler_params=_params(("arbitrary", "arbitrary"), 40),
        name="s5_in_proj",
    )(x, mod, gain, w)[0]


SB_BAND = 2 * LANES
SB_TAIL_ROWS = LANES


def _sb_log_terms(qq, kj, mask, u2, row_sum_on_lanes=False):
    z = _dot(qq, kj, NT_DIMS)
    log_beta = jnp.minimum(z, 0.0) - jnp.log(1.0 + jnp.exp(-jnp.abs(z)))
    log_keep = jnp.where(mask, log_beta - z, 0.0)
    hi, lo = _split_bf16(log_keep, 2)
    sums = _dot(hi, u2) + _dot(lo, u2)
    if not row_sum_on_lanes:
        return log_beta, sums[:, :SB_BAND], sums[:, SB_BAND:]
    ones = jnp.ones((SUBLANES, SB_BAND), BF16)
    return log_beta, sums[:, :SB_BAND], sums[:, SB_BAND:], _dot(ones, hi, NT_DIMS) + _dot(ones, lo, NT_DIMS)


def _sb_weights_dot_v(log_beta, later, carry, mask, vj):
    tail = later if carry is None else later + jnp.concatenate([carry] * (SB_BAND // LANES), axis=1)
    w = jnp.where(mask, jnp.exp(log_beta + tail), 0.0)
    return _dot(w.astype(BF16), vj)


def _sb_band(qq, kj, vj, mask, carry, u2):
    log_beta, later, row_sum = _sb_log_terms(qq, kj, mask, u2)
    pv = _sb_weights_dot_v(log_beta, later, carry, mask, vj)
    return pv, row_sum if carry is None else carry + row_sum


def _sb_attn_kernel(q_ref, k_ref, v_ref, u2_ref, ur_ref, o_ref, acc_ref, carry_ref, tacc_ref, tcarry_ref,
                    *, tq, dh, scale, pairs):
    q0 = pl.program_id(1) * tq
    rows = 2 * tq
    cap = SB_TAIL_ROWS
    u2 = u2_ref[...]
    lane = lax.broadcasted_iota(jnp.int32, (tq, LANES), 1)
    row = lax.broadcasted_iota(jnp.int32, (rows, SB_BAND), 0)
    col = lax.broadcasted_iota(jnp.int32, (rows, SB_BAND), 1)
    rel = jnp.where(row >= tq, row - tq, row) - col
    lanes = [slice(p * LANES, (p + 1) * LANES) for p in range(pairs)]

    qqs = []
    for p in range(pairs):
        qs = q_ref[0, :, lanes[p]] * jnp.asarray(scale, q_ref.dtype)
        zero = jnp.zeros_like(qs)
        qqs.append(jnp.concatenate([jnp.where(lane < dh, qs, zero), jnp.where(lane >= dh, qs, zero)], axis=0))

    off0 = pl.multiple_of(jnp.maximum(q0 + tq - SB_BAND, 0), LANES)
    causal = rel > (off0 - q0)
    slot = lax.broadcasted_iota(jnp.int32, (cap, rows), 0).astype(F32)
    ones = jnp.ones((rows, LANES), BF16)
    total = jnp.zeros((SUBLANES, LANES), F32)
    carry_max, carry_ts = None, []
    for p in range(pairs):
        kj = k_ref[0, pl.ds(off0, SB_BAND), lanes[p]]
        vj = v_ref[0, pl.ds(off0, SB_BAND), lanes[p]]
        log_beta, later, carry, carry_t = _sb_log_terms(qqs[p], kj, causal, u2, row_sum_on_lanes=True)
        acc_ref[p] = _sb_weights_dot_v(log_beta, later, None, causal, vj)
        carry_ref[p] = carry
        carry_ts.append(carry_t)
        carry_max = carry if carry_max is None else jnp.maximum(carry_max, carry)
    alives = [t > -SB_DEAD_LOG_WEIGHT for t in carry_ts]
    alive_bs = [jnp.where(a, 1.0, 0.0).astype(BF16) for a in alives]
    ranks = [_dot(a, ur_ref[...]) for a in alive_bs]
    counts = [_dot(a, ones) for a in alive_bs]
    c_c, sels = None, []
    for p in range(pairs):
        rank = ranks[p] + jnp.concatenate([total] * (rows // LANES), axis=1)
        hit = jnp.logical_and(rank[0:1, :] == slot, alives[p][0:1, :])
        sels.append(jnp.where(hit, 1.0, 0.0).astype(BF16))
        picked = jnp.sum(jnp.where(hit, carry_ts[p][0:1, :], 0.0), axis=1, keepdims=True)
        c_c = picked if c_c is None else c_c + picked
        total = total + counts[p]
    q_cs = [_dot(sels[p], qqs[p]) for p in range(pairs)]
    cmax0 = jnp.max(carry_max)
    n_live = jnp.max(total)

    def live(state):
        off_prev, cmax = state
        return jnp.logical_and(off_prev > 0, cmax > -SB_DEAD_LOG_WEIGHT)

    def earlier_offset(off_prev):
        return pl.multiple_of(jnp.maximum(off_prev - SB_BAND, 0), LANES)

    def walk_all_rows():
        def earlier_band(state):
            off_prev, _ = state
            off = earlier_offset(off_prev)
            fresh = col < (off_prev - off)
            cmax = None
            for p in range(pairs):
                kj = k_ref[0, pl.ds(off, SB_BAND), lanes[p]]
                vj = v_ref[0, pl.ds(off, SB_BAND), lanes[p]]
                pv, carry = _sb_band(qqs[p], kj, vj, fresh, carry_ref[p], u2)
                acc_ref[p] += pv
                carry_ref[p] = carry
                cmax = carry if cmax is None else jnp.maximum(cmax, carry)
            return off, jnp.max(cmax)

        lax.while_loop(live, earlier_band, (off0, cmax0))

    def walk_live_rows():
        q_c = jnp.concatenate(q_cs, axis=1).astype(BF16)
        used = lax.broadcasted_iota(jnp.int32, (cap, LANES), 0).astype(F32) < total[0:1, :]
        tcarry_ref[...] = jnp.where(used, jnp.broadcast_to(c_c, (cap, LANES)), -1e30)
        tacc_ref[...] = jnp.zeros_like(tacc_ref)
        col_c = lax.broadcasted_iota(jnp.int32, (cap, SB_BAND), 1)

        def two_earlier_bands(state):
            off_prev, _ = state
            off1 = earlier_offset(off_prev)
            off2 = earlier_offset(off1)
            fresh1 = col_c < (off_prev - off1)
            fresh2 = col_c < (off1 - off2)
            lb1, later1, sum1 = _sb_log_terms(q_c, k_ref[0, pl.ds(off1, SB_BAND), :], fresh1, u2)
            lb2, later2, sum2 = _sb_log_terms(q_c, k_ref[0, pl.ds(off2, SB_BAND), :], fresh2, u2)
            carry1 = tcarry_ref[...] + sum1
            pv1 = _sb_weights_dot_v(lb1, later1, tcarry_ref[...], fresh1, v_ref[0, pl.ds(off1, SB_BAND), :])
            pv2 = _sb_weights_dot_v(lb2, later2, carry1, fresh2, v_ref[0, pl.ds(off2, SB_BAND), :])
            tacc_ref[...] += pv1 + pv2
            carry2 = carry1 + sum2
            tcarry_ref[...] = carry2
            return off2, jnp.max(carry2)

        lax.while_loop(live, two_earlier_bands, (off0, cmax0))
        for p in range(pairs):
            add = None
            for part in _split_bf16(tacc_ref[:, lanes[p]], 2):
                d = _dot(sels[p], part, TN_DIMS)
                add = d if add is None else add + d
            acc_ref[p] += add

    more_keys = jnp.logical_and(off0 > 0, cmax0 > -SB_DEAD_LOG_WEIGHT)

    @pl.when(jnp.logical_and(more_keys, n_live <= cap))
    def _():
        walk_live_rows()

    @pl.when(jnp.logical_and(more_keys, n_live > cap))
    def _():
        walk_all_rows()

    for p in range(pairs):
        acc = acc_ref[p]
        o_ref[0, :, lanes[p]] = jnp.where(lane < dh, acc[:tq], acc[tq:]).astype(o_ref.dtype)


def _sb_attention(qkv, tq=LANES):
    b, s, _ = qkv.shape
    width = SB_HEADS * SB_HEAD_DIM
    pairs = width // LANES
    j = np.arange(SB_BAND)
    u2 = np.concatenate([(j[:, None] > j[None, :]).astype(np.float32), np.ones((SB_BAND, LANES), np.float32)],
                        axis=1)
    r = np.arange(2 * tq)
    ur = (r[:, None] < r[None, :]).astype(np.float32)
    return pl.pallas_call(
        functools.partial(_sb_attn_kernel, tq=tq, dh=SB_HEAD_DIM, scale=SB_HEAD_DIM ** -0.5, pairs=pairs),
        grid=(b, s // tq),
        in_specs=[pl.BlockSpec((1, tq, width), lambda bi, i: (bi, i, 0)),
                  pl.BlockSpec((1, s, width), lambda bi, i: (bi, 0, 1)),
                  pl.BlockSpec((1, s, width), lambda bi, i: (bi, 0, 2)),
                  pl.BlockSpec(u2.shape, lambda bi, i: (0, 0)),
                  pl.BlockSpec(ur.shape, lambda bi, i: (0, 0))],
        out_specs=pl.BlockSpec((1, tq, width), lambda bi, i: (bi, i, 0)),
        out_shape=jax.ShapeDtypeStruct((b, s, width), BF16),
        scratch_shapes=[pltpu.VMEM((pairs, 2 * tq, LANES), F32), pltpu.VMEM((pairs, 2 * tq, LANES), F32),
                        pltpu.VMEM((SB_TAIL_ROWS, width), F32), pltpu.VMEM((SB_TAIL_ROWS, LANES), F32)],
        compiler_params=_params(("arbitrary", "arbitrary"), 40),
        name="sb_attention",
    )(qkv, qkv, qkv, jnp.asarray(u2, BF16), jnp.asarray(ur, BF16))


HG_LEVELS = int(math.log2(HG_CHUNK))
HG_DIAG = HG_LEVELS


def _hgrn_tables():
    c = HG_CHUNK
    t = np.arange(c)[:, None]
    j = np.arange(c)[None, :]
    mats = [(j <= t), (j > t)]
    for lv in range(HG_LEVELS):
        m = 1 << lv
        mid = (t // (2 * m)) * (2 * m) + m - 1
        later = (t % (2 * m)) >= m
        mats.append(np.where(later, (j > mid) & (j <= t), (j > t) & (j <= mid)))
    stack = np.concatenate(mats, axis=0).astype(np.float32)
    s = j
    level = np.full((c, c), -1, np.int32)
    for lv in range(HG_LEVELS):
        level[(t > s) & ((t ^ s) >> lv == 1)] = lv
    level[t == s] = HG_DIAG
    return stack, level


def _hgrn_kernel(q_ref, f_ref, i_ref, g_ref, lbl_ref, gain_ref, gs_ref, lvl_ref, o_ref,
                 st_ref, ex_ref, sc_ref, kv_ref, acc_ref, *, layer, nch):
    ch = HG_CHUNK

    @pl.when(pl.program_id(1) == 0)
    def _():
        st_ref[...] = jnp.zeros_like(st_ref)

    logits = lbl_ref[...]
    ex = jnp.exp(logits - jnp.max(logits, axis=0, keepdims=True))
    lb = jnp.sum(ex[:layer + 1], axis=0, keepdims=True) / jnp.sum(ex, axis=0, keepdims=True)
    gs = gs_ref[...]
    lvl = lvl_ref[...]
    tiles = [(c, h, slice(c * ch, (c + 1) * ch), slice(h * HG_DIM, (h + 1) * HG_DIM))
             for c in range(nch) for h in range(HG_HEADS)]

    f = f_ref[0]
    e = jnp.exp(-jnp.abs(f))
    inv = 1.0 / (1.0 + e)
    sig_f = jnp.where(f >= 0, inv, e * inv)
    sig_nf = jnp.where(f >= 0, e * inv, inv)
    log_hi, log_lo = _split_bf16(jnp.log(lb + (1.0 - lb) * sig_f), 2)
    key = (1.0 - lb) * sig_nf
    for c in range(nch):
        rows = slice(c * ch, (c + 1) * ch)
        ex_c = jnp.exp(_dot(gs, log_hi[rows]) + _dot(gs, log_lo[rows]))
        for g in range(2 + HG_LEVELS):
            ex_ref[g, rows, :] = ex_c[g * ch:(g + 1) * ch]
    qa = _silu(q_ref[0])
    vb = i_ref[0].astype(BF16)
    cum_e = ex_ref[0]
    qc = (qa * cum_e).astype(BF16)
    kd = (key * ex_ref[1]).astype(BF16)
    qk = qa * key
    for c, h, rows, hs in tiles:
        sc_ref[c, h] = jnp.where(lvl == HG_DIAG, jnp.sum(qk[rows, hs], axis=1, keepdims=True), 0.0)
        kv_ref[c, h] = _dot(vb[rows, hs], kd[rows, hs], TN_DIMS)
    for lv in range(HG_LEVELS):
        p = ex_ref[2 + lv]
        qe = (qa * p).astype(BF16)
        ke = (key * p).astype(BF16)
        for c, h, rows, hs in tiles:
            s_lv = _dot(qe[rows, hs], ke[rows, hs], NT_DIMS)
            sc_ref[c, h] = jnp.where(lvl == lv, s_lv, sc_ref[c, h])
    for c, h, rows, hs in tiles:
        acc_ref[rows, hs] = _dot(sc_ref[c, h].astype(BF16), vb[rows, hs])
    for c, h, rows, hs in tiles:
        st = st_ref[h]
        kv_c = kv_ref[c, h]
        kv_ref[c, h] = st
        st_ref[h] = st * cum_e[(c + 1) * ch - 1:(c + 1) * ch, hs] + kv_c
    for c, h, rows, hs in tiles:
        acc_ref[rows, hs] += _dot(qc[rows, hs], kv_ref[c, h].astype(BF16), NT_DIMS)
    gate = gain_ref[...] * _silu(g_ref[0])
    for h in range(HG_HEADS):
        hs = slice(h * HG_DIM, (h + 1) * HG_DIM)
        o_ref[0, :, hs] = (_rms_scale(acc_ref[:, hs]) * gate[:, hs]).astype(o_ref.dtype)


def _hgrn2(hg, lb_logits, gain, layer, tb=512):
    b, s, _ = hg.shape
    w = HG_HEADS * HG_DIM
    stack, level = _hgrn_tables()
    blk = lambda k: pl.BlockSpec((1, tb, w), lambda bi, i: (bi, i, k))
    nch = tb // HG_CHUNK
    return pl.pallas_call(
        functools.partial(_hgrn_kernel, layer=layer, nch=nch),
        grid=(b, s // tb),
        in_specs=[blk(0), blk(1), blk(2), blk(3),
                  pl.BlockSpec(lb_logits.shape, lambda bi, i: (0, 0)),
                  pl.BlockSpec((1, w), lambda bi, i: (0, 0)),
                  pl.BlockSpec(stack.shape, lambda bi, i: (0, 0)),
                  pl.BlockSpec(level.shape, lambda bi, i: (0, 0))],
        out_specs=pl.BlockSpec((1, tb, w), lambda bi, i: (bi, i, 0)),
        out_shape=jax.ShapeDtypeStruct((b, s, w), BF16),
        scratch_shapes=[pltpu.VMEM((HG_HEADS, HG_DIM, HG_DIM), F32),
                        pltpu.VMEM((2 + HG_LEVELS, tb, w), F32),
                        pltpu.VMEM((nch, HG_HEADS, HG_CHUNK, HG_CHUNK), F32),
                        pltpu.VMEM((nch, HG_HEADS, HG_DIM, HG_DIM), F32),
                        pltpu.VMEM((tb, w), F32)],
        compiler_params=_params(("arbitrary", "arbitrary"), 40),
        name="hgrn2",
    )(hg, hg, hg, hg, lb_logits, gain, jnp.asarray(stack, BF16), jnp.asarray(level))


def _ffn_chunks(dff, width=512):
    edges = list(range(0, dff, width)) + [dff]
    return tuple(zip(edges[:-1], edges[1:]))


def _ffn_residual(x1, mod_ref, gain_ref, wg_ref, wu_ref, wo_ref, a_ref):
    h = _norm_mod(x1, gain_ref[...], mod_ref[0, 4:5, :], mod_ref[0, 3:4, :]).astype(BF16)
    for c0, c1 in _ffn_chunks(wg_ref.shape[1]):
        g = _dot(h, wg_ref[:, c0:c1])
        u = _dot(h, wu_ref[:, c0:c1])
        a_ref[:, c0:c1] = (_silu(g) * u).astype(BF16)
    return x1 + mod_ref[0, 5:6, :] * _dot(a_ref[...], wo_ref[...])


def _hybrid_out_ffn_kernel(x_ref, oa_ref, ob_ref, mod_ref, gain_ref, wa_ref, wb_ref, wg_ref, wu_ref, wo_ref,
                           o_ref, a_ref):
    mix = _dot(oa_ref[0], wa_ref[...]) + _dot(ob_ref[0], wb_ref[...])
    x1 = x_ref[0] + mod_ref[0, 2:3, :] * mix
    o_ref[0] = _ffn_residual(x1, mod_ref, gain_ref, wg_ref, wu_ref, wo_ref, a_ref)


def _hybrid_out_ffn(x, o_a, o_b, mod, gain_f, w_a, w_b, wg, wu, wo, tm=512):
    b, s, d = x.shape
    dff = wg.shape[1]
    row = lambda n: pl.BlockSpec((1, tm, n), lambda bi, i: (bi, i, 0))
    const = lambda shape: _resident(shape, lambda bi, i: (0, 0))
    return pl.pallas_call(
        _hybrid_out_ffn_kernel,
        grid=(b, s // tm),
        in_specs=[row(d), row(o_a.shape[2]), row(o_b.shape[2]),
                  pl.BlockSpec((1, N_MOD, d), lambda bi, i: (bi, 0, 0)),
                  pl.BlockSpec((1, d), lambda bi, i: (0, 0)),
                  const(w_a.shape), const(w_b.shape), const(wg.shape), const(wu.shape), const(wo.shape)],
        out_specs=row(d),
        out_shape=jax.ShapeDtypeStruct((b, s, d), F32),
        scratch_shapes=[pltpu.VMEM((tm, dff), BF16)],
        compiler_params=_params(("arbitrary", "arbitrary"), 56),
        name="hybrid_out_ffn",
    )(x, o_a, o_b, mod, gain_f, w_a, w_b, wg, wu, wo)


def _s5_glu_ffn_kernel(x_ref, y_ref, mod_ref, gain_ref, fin_ref, wv_ref, wt_ref, wg_ref, wu_ref, wo_ref,
                       o_ref, a_ref):
    y = y_ref[0]
    mix = _dot(y, wv_ref[...]) * _sigmoid(_dot(y, wt_ref[...]))
    x1 = x_ref[0] + mod_ref[0, 2:3, :] * mix
    x2 = _ffn_residual(x1, mod_ref, gain_ref, wg_ref, wu_ref, wo_ref, a_ref)
    o_ref[0] = _rms_scale(x2) * fin_ref[...]


def _s5_glu_ffn(x, y, mod, gain_f, final_g, wv, wt, wg, wu, wo, tm=512):
    b, s, d = x.shape
    dff = wg.shape[1]
    const = lambda shape: _resident(shape, lambda bi, i: (0, 0))
    return pl.pallas_call(
        _s5_glu_ffn_kernel,
        grid=(b, s // tm),
        in_specs=[pl.BlockSpec((1, tm, d), lambda bi, i: (bi, i, 0)),
                  pl.BlockSpec((1, tm, d), lambda bi, i: (bi, i, 0)),
                  pl.BlockSpec((1, N_MOD, d), lambda bi, i: (bi, 0, 0)),
                  pl.BlockSpec((1, d), lambda bi, i: (0, 0)),
                  pl.BlockSpec((1, d), lambda bi, i: (0, 0)),
                  const(wv.shape), const(wt.shape), const(wg.shape), const(wu.shape), const(wo.shape)],
        out_specs=pl.BlockSpec((1, tm, d), lambda bi, i: (bi, i, 0)),
        out_shape=jax.ShapeDtypeStruct((b, s, d), F32),
        scratch_shapes=[pltpu.VMEM((tm, dff), BF16)],
        compiler_params=_params(("arbitrary", "arbitrary"), 56),
        name="s5_glu_ffn",
    )(x, y, mod, gain_f, final_g, wv, wt, wg, wu, wo)


def _s5_discretise_kernel(lr_ref, li_ref, ldt_ref, br_ref, bi_ref, are_ref, aim_ref, bbr_ref, bbi_ref):
    lr = jnp.minimum(lr_ref[...], -1e-4)
    li = li_ref[...]
    dt = jnp.exp(ldt_ref[...])
    mag = jnp.exp(lr * dt)
    a_re = mag * jnp.cos(li * dt)
    a_im = mag * jnp.sin(li * dt)
    e_re = a_re - 1.0
    den = lr * lr + li * li
    z_re = (e_re * lr + a_im * li) / den
    z_im = (a_im * lr - e_re * li) / den
    are_ref[...] = a_re
    aim_ref[...] = a_im
    br = br_ref[...]
    bi = bi_ref[...]
    bbr_ref[...] = z_re[:, None, :] * br - z_im[:, None, :] * bi
    bbi_ref[...] = z_re[:, None, :] * bi + z_im[:, None, :] * br


def _s5_discretise(lam_re, lam_im, log_dt, b_re_t, b_im_t):
    g, p = lam_re.shape
    return pl.pallas_call(
        _s5_discretise_kernel,
        out_shape=[jax.ShapeDtypeStruct((g, p), F32), jax.ShapeDtypeStruct((g, p), F32),
                   jax.ShapeDtypeStruct(b_re_t.shape, F32), jax.ShapeDtypeStruct(b_re_t.shape, F32)],
        name="s5_discretise",
    )(lam_re, lam_im, log_dt.reshape(g, 1), b_re_t, b_im_t)


def _gelu_tanh(x):
    return 0.5 * x * (1.0 + jnp.tanh(math.sqrt(2.0 / math.pi) * (x + 0.044715 * (x * x * x))))


def _s5_scan_kernel(u_ref, bm_ref, cr_ref, ci_ref, are_ref, aim_ref, d_ref, o_ref,
                    bu_ref, xr_ref, xi_ref, st_ref, *, ts, nb, half):
    @pl.when(pl.program_id(1) == 0)
    def _():
        st_ref[...] = jnp.zeros_like(st_ref)

    u = jnp.swapaxes(u_ref[...], 0, 1).reshape(ts * nb, LANES)
    bu_ref[...] = _dot(u.astype(BF16), bm_ref[0])
    a_re = jnp.broadcast_to(are_ref[0], (nb, half))
    a_im = jnp.broadcast_to(aim_ref[0], (nb, half))

    def step(t, state):
        x_re, x_im = state
        r0 = pl.multiple_of(t * nb, nb)
        n_re = a_re * x_re - a_im * x_im + bu_ref[pl.ds(r0, nb), 0:half]
        n_im = a_re * x_im + a_im * x_re + bu_ref[pl.ds(r0, nb), half:2 * half]
        xr_ref[pl.ds(r0, nb), :] = n_re
        xi_ref[pl.ds(r0, nb), :] = n_im
        return n_re, n_im

    x_re, x_im = lax.fori_loop(0, ts, step, (st_ref[:, 0:half], st_ref[:, half:2 * half]), unroll=8)
    st_ref[:, 0:half] = x_re
    st_ref[:, half:2 * half] = x_im
    y = _dot(xr_ref[...].astype(BF16), cr_ref[0]) - _dot(xi_ref[...].astype(BF16), ci_ref[0])
    y = _gelu_tanh(y + d_ref[0] * u)
    o_ref[...] = jnp.swapaxes(y.reshape(ts, nb, LANES), 0, 1).astype(o_ref.dtype)


def _s5_scan(u, bmat, cr, ci, a_re, a_im, d_skip, ts=128):
    nb, s, d = u.shape
    slabs = d // LANES
    half = bmat.shape[2] // 2
    io = pl.BlockSpec((nb, ts, LANES), lambda o, i: (0, i, o))
    per_slab = lambda shape: pl.BlockSpec((1,) + shape, lambda o, i: (o, 0, 0))
    return pl.pallas_call(
        functools.partial(_s5_scan_kernel, ts=ts, nb=nb, half=half),
        grid=(slabs, s // ts),
        in_specs=[io, per_slab(bmat.shape[1:]), per_slab(cr.shape[1:]), per_slab(ci.shape[1:]),
                  per_slab((1, half)), per_slab((1, half)), per_slab((1, LANES))],
        out_specs=io,
        out_shape=jax.ShapeDtypeStruct((nb, s, d), BF16),
        scratch_shapes=[pltpu.VMEM((ts * nb, 2 * half), F32), pltpu.VMEM((ts * nb, half), F32),
                        pltpu.VMEM((ts * nb, half), F32), pltpu.VMEM((nb, 2 * half), F32)],
        compiler_params=_params(("arbitrary", "arbitrary"), 40),
        name="s5_scan",
    )(u, bmat, cr, ci, a_re, a_im, d_skip)


def _block_diag_slabs(t, per_slab):
    g, a, b = t.shape
    t = t.reshape(g // per_slab, per_slab, a, b)
    eye = jnp.eye(per_slab, dtype=t.dtype)
    return jnp.einsum('ogab,gk->ogakb', t, eye).reshape(g // per_slab, per_slab * a, per_slab * b)


def kernel(x, c, norm_mix_g, norm_ffn_g, ada_w, ada_b, ffn_w_in, ffn_w_out, final_norm_g, hy_w_in, hy_w_out, hg_norm_g, hg_lb_logits, s5_w_in, s5_lam_re, s5_lam_im, s5_log_dt, s5_b_re, s5_b_im, s5_c_re, s5_c_im, s5_d, s5_w_glu):
    b, s, d = x.shape
    depth = ada_w.shape[0]
    dff = ffn_w_out.shape[1]
    sb_w = SB_HEADS * SB_HEAD_DIM
    mod = _adaln_mod(c, ada_w, ada_b).reshape(depth, b, N_MOD, d)
    per_slab = LANES // S5_GROUP

    for layer in range(depth):
        i = layer // 2
        wg = ffn_w_in[layer, :, :dff].astype(BF16)
        wu = ffn_w_in[layer, :, dff:].astype(BF16)
        wo = ffn_w_out[layer].astype(BF16)
        gain_m = norm_mix_g[layer].reshape(1, d)
        gain_f = norm_ffn_g[layer].reshape(1, d)
        if layer % 2 == 0:
            qkv, hg = _hybrid_in_proj(x, mod[layer], gain_m, hy_w_in[i].astype(BF16))
            o_a = _sb_attention(qkv)
            o_b = _hgrn2(hg, hg_lb_logits, hg_norm_g[i].reshape(1, -1), layer)
            w_out = hy_w_out[i].astype(BF16)
            x = _hybrid_out_ffn(x, o_a, o_b, mod[layer], gain_f, w_out[:sb_w], w_out[sb_w:], wg, wu, wo)
        else:
            u = _s5_in_proj(x, mod[layer], gain_m, s5_w_in[i].astype(BF16))
            a_re, a_im, bb_re, bb_im = _s5_discretise(
                s5_lam_re[i], s5_lam_im[i], s5_log_dt[i],
                jnp.swapaxes(s5_b_re[i], 1, 2), jnp.swapaxes(s5_b_im[i], 1, 2))
            slabs = d // LANES
            bmat = jnp.concatenate([_block_diag_slabs(bb_re, per_slab), _block_diag_slabs(bb_im, per_slab)],
                                   axis=2).astype(BF16)
            cr = _block_diag_slabs(jnp.swapaxes(s5_c_re[i], 1, 2), per_slab).astype(BF16)
            ci = _block_diag_slabs(jnp.swapaxes(s5_c_im[i], 1, 2), per_slab).astype(BF16)
            half = per_slab * S5_STATE
            y = _s5_scan(u, bmat, cr, ci, a_re.reshape(slabs, 1, half), a_im.reshape(slabs, 1, half),
                         s5_d[i].reshape(slabs, 1, LANES))
            w_glu = s5_w_glu[i].astype(BF16)
            last = layer == depth - 1
            x = _s5_glu_ffn(x, y, mod[layer], gain_f, final_norm_g.reshape(1, d),
                            w_glu[:, :d], w_glu[:, d:], wg, wu, wo)
            assert last, "final norm is fused into the S5 layer's FFN kernel"
    return x
```

```python
import functools
import math

import numpy as np
import jax
import jax.numpy as jnp
from jax import lax
from jax.experimental import pallas as pl
from jax.experimental.pallas import tpu as pltpu

F32 = jnp.float32
BF16 = jnp.bfloat16

EPS = 1e-6
N_MOD = 6
SB_HEADS = 8
SB_HEAD_DIM = 64
HG_HEADS = 4
HG_DIM = 128
HG_CHUNK = 64
S5_GROUP = 16
S5_STATE = 64

LANES = 128
SUBLANES = 8
MXU_DIM = 256
MIB = 1 << 20

SB_DEAD_LOG_WEIGHT = 88.0

NT_DIMS = (((1,), (1,)), ((), ()))
TN_DIMS = (((0,), (0,)), ((), ()))


def _dot(a, b, dims=None):
    if dims is None:
        return jnp.dot(a, b, preferred_element_type=F32)
    return lax.dot_general(a, b, dims, preferred_element_type=F32)


def _split_bf16(x, n):
    parts = []
    rest = x
    for i in range(n):
        p = rest.astype(BF16)
        parts.append(p)
        if i + 1 < n:
            rest = rest - p.astype(F32)
    return parts


def _sigmoid(x):
    return 1.0 / (1.0 + jnp.exp(-x))


def _silu(x):
    return x * _sigmoid(x)


def _rms_scale(x):
    return x * lax.rsqrt(jnp.mean(x * x, axis=-1, keepdims=True) + EPS)


def _norm_mod(x, gain, scale, shift):
    return _rms_scale(x) * (gain * (1.0 + scale)) + shift


def _params(semantics, vmem_mib):
    return pltpu.CompilerParams(dimension_semantics=semantics, vmem_limit_bytes=vmem_mib * MIB)


def _resident(shape, index_map):
    return pl.BlockSpec(shape, index_map, pipeline_mode=pl.Buffered(1))


def _mod_kernel(c_ref, w_ref, b_ref, o_ref):
    a_hi, a_lo = _split_bf16(_silu(c_ref[...]), 2)
    w_hi, w_lo = _split_bf16(w_ref[0], 2)
    acc = _dot(a_hi, w_hi) + _dot(a_hi, w_lo) + _dot(a_lo, w_hi)
    o_ref[0] = acc + b_ref[0]


def _adaln_mod(c, ada_w, ada_b, tn=1536):
    depth, d, n = ada_w.shape
    b = c.shape[0]
    return pl.pallas_call(
        _mod_kernel,
        grid=(depth, n // tn),
        in_specs=[pl.BlockSpec((b, d), lambda l, j: (0, 0)),
                  pl.BlockSpec((1, d, tn), lambda l, j: (l, 0, j)),
                  pl.BlockSpec((1, 1, tn), lambda l, j: (l, 0, j))],
        out_specs=pl.BlockSpec((1, b, tn), lambda l, j: (l, 0, j)),
        out_shape=jax.ShapeDtypeStruct((depth, b, n), F32),
        compiler_params=_params(("arbitrary", "arbitrary"), 40),
        name="adaln_mod",
    )(c, ada_w, ada_b.reshape(depth, 1, n))


def _normmod_proj_kernel(x_ref, mod_ref, gain_ref, w_ref, *o_refs, splits):
    x = x_ref[0]
    h = _norm_mod(x, gain_ref[...], mod_ref[0, 1:2, :], mod_ref[0, 0:1, :]).astype(BF16)
    for o_ref, (c0, c1) in zip(o_refs, splits):
        y = _dot(h, w_ref[:, c0:c1])
        o_ref[...] = y.reshape(o_ref.shape).astype(o_ref.dtype)


def _hybrid_in_proj(x, mod, gain, w, tm=512):
    b, s, d = x.shape
    n = w.shape[1]
    n_a = 3 * SB_HEADS * SB_HEAD_DIM
    return pl.pallas_call(
        functools.partial(_normmod_proj_kernel, splits=((0, n_a), (n_a, n))),
        grid=(b, s // tm),
        in_specs=[pl.BlockSpec((1, tm, d), lambda bi, i: (bi, i, 0)),
                  pl.BlockSpec((1, N_MOD, d), lambda bi, i: (bi, 0, 0)),
                  pl.BlockSpec((1, d), lambda bi, i: (0, 0)),
                  _resident((d, n), lambda bi, i: (0, 0))],
        out_specs=[pl.BlockSpec((1, tm, n_a), lambda bi, i: (bi, i, 0)),
                   pl.BlockSpec((1, tm, n - n_a), lambda bi, i: (bi, i, 0))],
        out_shape=[jax.ShapeDtypeStruct((b, s, n_a), BF16),
                   jax.ShapeDtypeStruct((b, s, n - n_a), F32)],
        compiler_params=_params(("arbitrary", "arbitrary"), 48),
        name="hybrid_in_proj",
    )(x, mod, gain, w)


def _s5_in_proj(x, mod, gain, w, tm=512):
    b, s, d = x.shape
    return pl.pallas_call(
        functools.partial(_normmod_proj_kernel, splits=((0, d),)),
        grid=(b, s // tm),
        in_specs=[pl.BlockSpec((1, tm, d), lambda bi, i: (bi, i, 0)),
                  pl.BlockSpec((1, N_MOD, d), lambda bi, i: (bi, 0, 0)),
                  pl.BlockSpec((1, d), lambda bi, i: (0, 0)),
                  _resident((d, d), lambda bi, i: (0, 0))],
        out_specs=[pl.BlockSpec((1, tm, d), lambda bi, i: (bi, i, 0))],
        out_shape=[jax.ShapeDtypeStruct((b, s, d), F32)],
        compiler_params=_params(("arbitrary", "arbitrary"), 40),
        name="s5_in_proj",
    )(x, mod, gain, w)[0]


SB_BAND = 2 * LANES
SB_TAIL_ROWS = LANES


def _sb_log_terms(qq, kj, mask, u2, row_sum_on_lanes=False):
    z = _dot(qq, kj, NT_DIMS)
    log_beta = jnp.minimum(z, 0.0) - jnp.log(1.0 + jnp.exp(-jnp.abs(z)))
    log_keep = jnp.where(mask, log_beta - z, 0.0)
    hi, lo = _split_bf16(log_keep, 2)
    sums = _dot(hi, u2) + _dot(lo, u2)
    if not row_sum_on_lanes:
        return log_beta, sums[:, :SB_BAND], sums[:, SB_BAND:]
    ones = jnp.ones((SUBLANES, SB_BAND), BF16)
    return log_beta, sums[:, :SB_BAND], sums[:, SB_BAND:], _dot(ones, hi, NT_DIMS) + _dot(ones, lo, NT_DIMS)


def _sb_weights_dot_v(log_beta, later, carry, mask, vj):
    tail = later if carry is None else later + jnp.concatenate([carry] * (SB_BAND // LANES), axis=1)
    w = jnp.where(mask, jnp.exp(log_beta + tail), 0.0)
    return _dot(w.astype(BF16), vj)


def _sb_band(qq, kj, vj, mask, carry, u2):
    log_beta, later, row_sum = _sb_log_terms(qq, kj, mask, u2)
    pv = _sb_weights_dot_v(log_beta, later, carry, mask, vj)
    return pv, row_sum if carry is None else carry + row_sum


def _sb_attn_kernel(q_ref, k_ref, v_ref, u2_ref, ur_ref, o_ref, acc_ref, carry_ref, tacc_ref, tcarry_ref,
                    *, tq, dh, scale, pairs):
    q0 = pl.program_id(1) * tq
    rows = 2 * tq
    cap = SB_TAIL_ROWS
    u2 = u2_ref[...]
    lane = lax.broadcasted_iota(jnp.int32, (tq, LANES), 1)
    row = lax.broadcasted_iota(jnp.int32, (rows, SB_BAND), 0)
    col = lax.broadcasted_iota(jnp.int32, (rows, SB_BAND), 1)
    rel = jnp.where(row >= tq, row - tq, row) - col
    lanes = [slice(p * LANES, (p + 1) * LANES) for p in range(pairs)]

    qqs = []
    for p in range(pairs):
        qs = q_ref[0, :, lanes[p]] * jnp.asarray(scale, q_ref.dtype)
        zero = jnp.zeros_like(qs)
        qqs.append(jnp.concatenate([jnp.where(lane < dh, qs, zero), jnp.where(lane >= dh, qs, zero)], axis=0))

    off0 = pl.multiple_of(jnp.maximum(q0 + tq - SB_BAND, 0), LANES)
    causal = rel > (off0 - q0)
    slot = lax.broadcasted_iota(jnp.int32, (cap, rows), 0).astype(F32)
    ones = jnp.ones((rows, LANES), BF16)
    total = jnp.zeros((SUBLANES, LANES), F32)
    carry_max, carry_ts = None, []
    for p in range(pairs):
        kj = k_ref[0, pl.ds(off0, SB_BAND), lanes[p]]
        vj = v_ref[0, pl.ds(off0, SB_BAND), lanes[p]]
        log_beta, later, carry, carry_t = _sb_log_terms(qqs[p], kj, causal, u2, row_sum_on_lanes=True)
        acc_ref[p] = _sb_weights_dot_v(log_beta, later, None, causal, vj)
        carry_ref[p] = carry
        carry_ts.append(carry_t)
        carry_max = carry if carry_max is None else jnp.maximum(carry_max, carry)
    alives = [t > -SB_DEAD_LOG_WEIGHT for t in carry_ts]
    alive_bs = [jnp.where(a, 1.0, 0.0).astype(BF16) for a in alives]
    ranks = [_dot(a, ur_ref[...]) for a in alive_bs]
    counts = [_dot(a, ones) for a in alive_bs]
    c_c, sels = None, []
    for p in range(pairs):
        rank = ranks[p] + jnp.concatenate([total] * (rows // LANES), axis=1)
        hit = jnp.logical_and(rank[0:1, :] == slot, alives[p][0:1, :])
        sels.append(jnp.where(hit, 1.0, 0.0).astype(BF16))
        picked = jnp.sum(jnp.where(hit, carry_ts[p][0:1, :], 0.0), axis=1, keepdims=True)
        c_c = picked if c_c is None else c_c + picked
        total = total + counts[p]
    q_cs = [_dot(sels[p], qqs[p]) for p in range(pairs)]
    cmax0 = jnp.max(carry_max)
    n_live = jnp.max(total)

    def live(state):
        off_prev, cmax = state
        return jnp.logical_and(off_prev > 0, cmax > -SB_DEAD_LOG_WEIGHT)

    def earlier_offset(off_prev):
        return pl.multiple_of(jnp.maximum(off_prev - SB_BAND, 0), LANES)

    def walk_all_rows():
        def earlier_band(state):
            off_prev, _ = state
            off = earlier_offset(off_prev)
            fresh = col < (off_prev - off)
            cmax = None
            for p in range(pairs):
                kj = k_ref[0, pl.ds(off, SB_BAND), lanes[p]]
                vj = v_ref[0, pl.ds(off, SB_BAND), lanes[p]]
                pv, carry = _sb_band(qqs[p], kj, vj, fresh, carry_ref[p], u2)
                acc_ref[p] += pv
                carry_ref[p] = carry
                cmax = carry if cmax is None else jnp.maximum(cmax, carry)
            return off, jnp.max(cmax)

        lax.while_loop(live, earlier_band, (off0, cmax0))

    def walk_live_rows():
        q_c = jnp.concatenate(q_cs, axis=1).astype(BF16)
        used = lax.broadcasted_iota(jnp.int32, (cap, LANES), 0).astype(F32) < total[0:1, :]
        tcarry_ref[...] = jnp.where(used, jnp.broadcast_to(c_c, (cap, LANES)), -1e30)
        tacc_ref[...] = jnp.zeros_like(tacc_ref)
        col_c = lax.broadcasted_iota(jnp.int32, (cap, SB_BAND), 1)

        def two_earlier_bands(state):
            off_prev, _ = state
            off1 = earlier_offset(off_prev)
            off2 = earlier_offset(off1)
            fresh1 = col_c < (off_prev - off1)
            fresh2 = col_c < (off1 - off2)
            lb1, later1, sum1 = _sb_log_terms(q_c, k_ref[0, pl.ds(off1, SB_BAND), :], fresh1, u2)
            lb2, later2, sum2 = _sb_log_terms(q_c, k_ref[0, pl.ds(off2, SB_BAND), :], fresh2, u2)
            carry1 = tcarry_ref[...] + sum1
            pv1 = _sb_weights_dot_v(lb1, later1, tcarry_ref[...], fresh1, v_ref[0, pl.ds(off1, SB_BAND), :])
            pv2 = _sb_weights_dot_v(lb2, later2, carry1, fresh2, v_ref[0, pl.ds(off2, SB_BAND), :])
            tacc_ref[...] += pv1 + pv2
            carry2 = carry1 + sum2
            tcarry_ref[...] = carry2
            return off2, jnp.max(carry2)

        lax.while_loop(live, two_earlier_bands, (off0, cmax0))
        for p in range(pairs):
            add = None
            for part in _split_bf16(tacc_ref[:, lanes[p]], 2):
                d = _dot(sels[p], part, TN_DIMS)
                add = d if add is None else add + d
            acc_ref[p] += add

    more_keys = jnp.logical_and(off0 > 0, cmax0 > -SB_DEAD_LOG_WEIGHT)

    @pl.when(jnp.logical_and(more_keys, n_live <= cap))
    def _():
        walk_live_rows()

    @pl.when(jnp.logical_and(more_keys, n_live > cap))
    def _():
        walk_all_rows()

    for p in range(pairs):
        acc = acc_ref[p]
        o_ref[0, :, lanes[p]] = jnp.where(lane < dh, acc[:tq], acc[tq:]).astype(o_ref.dtype)


def _sb_attention(qkv, tq=LANES):
    b, s, _ = qkv.shape
    width = SB_HEADS * SB_HEAD_DIM
    pairs = width // LANES
    j = np.arange(SB_BAND)
    u2 = np.concatenate([(j[:, None] > j[None, :]).astype(np.float32), np.ones((SB_BAND, LANES), np.float32)],
                        axis=1)
    r = np.arange(2 * tq)
    ur = (r[:, None] < r[None, :]).astype(np.float32)
    return pl.pallas_call(
        functools.partial(_sb_attn_kernel, tq=tq, dh=SB_HEAD_DIM, scale=SB_HEAD_DIM ** -0.5, pairs=pairs),
        grid=(b, s // tq),
        in_specs=[pl.BlockSpec((1, tq, width), lambda bi, i: (bi, i, 0)),
                  pl.BlockSpec((1, s, width), lambda bi, i: (bi, 0, 1)),
                  pl.BlockSpec((1, s, width), lambda bi, i: (bi, 0, 2)),
                  pl.BlockSpec(u2.shape, lambda bi, i: (0, 0)),
                  pl.BlockSpec(ur.shape, lambda bi, i: (0, 0))],
        out_specs=pl.BlockSpec((1, tq, width), lambda bi, i: (bi, i, 0)),
        out_shape=jax.ShapeDtypeStruct((b, s, width), BF16),
        scratch_shapes=[pltpu.VMEM((pairs, 2 * tq, LANES), F32), pltpu.VMEM((pairs, 2 * tq, LANES), F32),
                        pltpu.VMEM((SB_TAIL_ROWS, width), F32), pltpu.VMEM((SB_TAIL_ROWS, LANES), F32)],
        compiler_params=_params(("arbitrary", "arbitrary"), 40),
        name="sb_attention",
    )(qkv, qkv, qkv, jnp.asarray(u2, BF16), jnp.asarray(ur, BF16))


HG_LEVELS = int(math.log2(HG_CHUNK))
HG_DIAG = HG_LEVELS


def _hgrn_tables():
    c = HG_CHUNK
    t = np.arange(c)[:, None]
    j = np.arange(c)[None, :]
    mats = [(j <= t), (j > t)]
    for lv in range(HG_LEVELS):
        m = 1 << lv
        mid = (t // (2 * m)) * (2 * m) + m - 1
        later = (t % (2 * m)) >= m
        mats.append(np.where(later, (j > mid) & (j <= t), (j > t) & (j <= mid)))
    stack = np.concatenate(mats, axis=0).astype(np.float32)
    s = j
    level = np.full((c, c), -1, np.int32)
    for lv in range(HG_LEVELS):
        level[(t > s) & ((t ^ s) >> lv == 1)] = lv
    level[t == s] = HG_DIAG
    return stack, level


def _hgrn_kernel(q_ref, f_ref, i_ref, g_ref, lbl_ref, gain_ref, gs_ref, lvl_ref, o_ref,
                 st_ref, ex_ref, sc_ref, kv_ref, acc_ref, *, layer, nch):
    ch = HG_CHUNK

    @pl.when(pl.program_id(1) == 0)
    def _():
        st_ref[...] = jnp.zeros_like(st_ref)

    logits = lbl_ref[...]
    ex = jnp.exp(logits - jnp.max(logits, axis=0, keepdims=True))
    lb = jnp.sum(ex[:layer + 1], axis=0, keepdims=True) / jnp.sum(ex, axis=0, keepdims=True)
    gs = gs_ref[...]
    lvl = lvl_ref[...]
    tiles = [(c, h, slice(c * ch, (c + 1) * ch), slice(h * HG_DIM, (h + 1) * HG_DIM))
             for c in range(nch) for h in range(HG_HEADS)]

    f = f_ref[0]
    e = jnp.exp(-jnp.abs(f))
    inv = 1.0 / (1.0 + e)
    sig_f = jnp.where(f >= 0, inv, e * inv)
    sig_nf = jnp.where(f >= 0, e * inv, inv)
    log_hi, log_lo = _split_bf16(jnp.log(lb + (1.0 - lb) * sig_f), 2)
    key = (1.0 - lb) * sig_nf
    for c in range(nch):
        rows = slice(c * ch, (c + 1) * ch)
        ex_c = jnp.exp(_dot(gs, log_hi[rows]) + _dot(gs, log_lo[rows]))
        for g in range(2 + HG_LEVELS):
            ex_ref[g, rows, :] = ex_c[g * ch:(g + 1) * ch]
    qa = _silu(q_ref[0])
    vb = i_ref[0].astype(BF16)
    cum_e = ex_ref[0]
    qc = (qa * cum_e).astype(BF16)
    kd = (key * ex_ref[1]).astype(BF16)
    qk = qa * key
    for c, h, rows, hs in tiles:
        sc_ref[c, h] = jnp.where(lvl == HG_DIAG, jnp.sum(qk[rows, hs], axis=1, keepdims=True), 0.0)
        kv_ref[c, h] = _dot(vb[rows, hs], kd[rows, hs], TN_DIMS)
    for lv in range(HG_LEVELS):
        p = ex_ref[2 + lv]
        qe = (qa * p).astype(BF16)
        ke = (key * p).astype(BF16)
        for c, h, rows, hs in tiles:
            s_lv = _dot(qe[rows, hs], ke[rows, hs], NT_DIMS)
            sc_ref[c, h] = jnp.where(lvl == lv, s_lv, sc_ref[c, h])
    for c, h, rows, hs in tiles:
        acc_ref[rows, hs] = _dot(sc_ref[c, h].astype(BF16), vb[rows, hs])
    for c, h, rows, hs in tiles:
        st = st_ref[h]
        kv_c = kv_ref[c, h]
        kv_ref[c, h] = st
        st_ref[h] = st * cum_e[(c + 1) * ch - 1:(c + 1) * ch, hs] + kv_c
    for c, h, rows, hs in tiles:
        acc_ref[rows, hs] += _dot(qc[rows, hs], kv_ref[c, h].astype(BF16), NT_DIMS)
    gate = gain_ref[...] * _silu(g_ref[0])
    for h in range(HG_HEADS):
        hs = slice(h * HG_DIM, (h + 1) * HG_DIM)
        o_ref[0, :, hs] = (_rms_scale(acc_ref[:, hs]) * gate[:, hs]).astype(o_ref.dtype)


def _hgrn2(hg, lb_logits, gain, layer, tb=512):
    b, s, _ = hg.shape
    w = HG_HEADS * HG_DIM
    stack, level = _hgrn_tables()
    blk = lambda k: pl.BlockSpec((1, tb, w), lambda bi, i: (bi, i, k))
    nch = tb // HG_CHUNK
    return pl.pallas_call(
        functools.partial(_hgrn_kernel, layer=layer, nch=nch),
        grid=(b, s // tb),
        in_specs=[blk(0), blk(1), blk(2), blk(3),
                  pl.BlockSpec(lb_logits.shape, lambda bi, i: (0, 0)),
                  pl.BlockSpec((1, w), lambda bi, i: (0, 0)),
                  pl.BlockSpec(stack.shape, lambda bi, i: (0, 0)),
                  pl.BlockSpec(level.shape, lambda bi, i: (0, 0))],
        out_specs=pl.BlockSpec((1, tb, w), lambda bi, i: (bi, i, 0)),
        out_shape=jax.ShapeDtypeStruct((b, s, w), BF16),
        scratch_shapes=[pltpu.VMEM((HG_HEADS, HG_DIM, HG_DIM), F32),
                        pltpu.VMEM((2 + HG_LEVELS, tb, w), F32),
                        pltpu.VMEM((nch, HG_HEADS, HG_CHUNK, HG_CHUNK), F32),
                        pltpu.VMEM((nch, HG_HEADS, HG_DIM, HG_DIM), F32),
                        pltpu.VMEM((tb, w), F32)],
        compiler_params=_params(("arbitrary", "arbitrary"), 40),
        name="hgrn2",
    )(hg, hg, hg, hg, lb_logits, gain, jnp.asarray(stack, BF16), jnp.asarray(level))


def _ffn_chunks(dff, width=512):
    edges = list(range(0, dff, width)) + [dff]
    return tuple(zip(edges[:-1], edges[1:]))


def _ffn_residual(x1, mod_ref, gain_ref, wg_ref, wu_ref, wo_ref, a_ref):
    h = _norm_mod(x1, gain_ref[...], mod_ref[0, 4:5, :], mod_ref[0, 3:4, :]).astype(BF16)
    for c0, c1 in _ffn_chunks(wg_ref.shape[1]):
        g = _dot(h, wg_ref[:, c0:c1])
        u = _dot(h, wu_ref[:, c0:c1])
        a_ref[:, c0:c1] = (_silu(g) * u).astype(BF16)
    return x1 + mod_ref[0, 5:6, :] * _dot(a_ref[...], wo_ref[...])


def _hybrid_out_ffn_kernel(x_ref, oa_ref, ob_ref, mod_ref, gain_ref, wa_ref, wb_ref, wg_ref, wu_ref, wo_ref,
                           o_ref, a_ref):
    mix = _dot(oa_ref[0], wa_ref[...]) + _dot(ob_ref[0], wb_ref[...])
    x1 = x_ref[0] + mod_ref[0, 2:3, :] * mix
    o_ref[0] = _ffn_residual(x1, mod_ref, gain_ref, wg_ref, wu_ref, wo_ref, a_ref)


def _hybrid_out_ffn(x, o_a, o_b, mod, gain_f, w_a, w_b, wg, wu, wo, tm=512):
    b, s, d = x.shape
    dff = wg.shape[1]
    row = lambda n: pl.BlockSpec((1, tm, n), lambda bi, i: (bi, i, 0))
    const = lambda shape: _resident(shape, lambda bi, i: (0, 0))
    return pl.pallas_call(
        _hybrid_out_ffn_kernel,
        grid=(b, s // tm),
        in_specs=[row(d), row(o_a.shape[2]), row(o_b.shape[2]),
                  pl.BlockSpec((1, N_MOD, d), lambda bi, i: (bi, 0, 0)),
                  pl.BlockSpec((1, d), lambda bi, i: (0, 0)),
                  const(w_a.shape), const(w_b.shape), const(wg.shape), const(wu.shape), const(wo.shape)],
        out_specs=row(d),
        out_shape=jax.ShapeDtypeStruct((b, s, d), F32),
        scratch_shapes=[pltpu.VMEM((tm, dff), BF16)],
        compiler_params=_params(("arbitrary", "arbitrary"), 56),
        name="hybrid_out_ffn",
    )(x, o_a, o_b, mod, gain_f, w_a, w_b, wg, wu, wo)


def _s5_glu_ffn_kernel(x_ref, y_ref, mod_ref, gain_ref, fin_ref, wv_ref, wt_ref, wg_ref, wu_ref, wo_ref,
                       o_ref, a_ref):
    y = y_ref[0]
    mix = _dot(y, wv_ref[...]) * _sigmoid(_dot(y, wt_ref[...]))
    x1 = x_ref[0] + mod_ref[0, 2:3, :] * mix
    x2 = _ffn_residual(x1, mod_ref, gain_ref, wg_ref, wu_ref, wo_ref, a_ref)
    o_ref[0] = _rms_scale(x2) * fin_ref[...]


def _s5_glu_ffn(x, y, mod, gain_f, final_g, wv, wt, wg, wu, wo, tm=512):
    b, s, d = x.shape
    dff = wg.shape[1]
    const = lambda shape: _resident(shape, lambda bi, i: (0, 0))
    return pl.pallas_call(
        _s5_glu_ffn_kernel,
        grid=(b, s // tm),
        in_specs=[pl.BlockSpec((1, tm, d), lambda bi, i: (bi, i, 0)),
                  pl.BlockSpec((1, tm, d), lambda bi, i: (bi, i, 0)),
                  pl.BlockSpec((1, N_MOD, d), lambda bi, i: (bi, 0, 0)),
                  pl.BlockSpec((1, d), lambda bi, i: (0, 0)),
                  pl.BlockSpec((1, d), lambda bi, i: (0, 0)),
                  const(wv.shape), const(wt.shape), const(wg.shape), const(wu.shape), const(wo.shape)],
        out_specs=pl.BlockSpec((1, tm, d), lambda bi, i: (bi, i, 0)),
        out_shape=jax.ShapeDtypeStruct((b, s, d), F32),
        scratch_shapes=[pltpu.VMEM((tm, dff), BF16)],
        compiler_params=_params(("arbitrary", "arbitrary"), 56),
        name="s5_glu_ffn",
    )(x, y, mod, gain_f, final_g, wv, wt, wg, wu, wo)


def _s5_discretise_kernel(lr_ref, li_ref, ldt_ref, br_ref, bi_ref, are_ref, aim_ref, bbr_ref, bbi_ref):
    lr = jnp.minimum(lr_ref[...], -1e-4)
    li = li_ref[...]
    dt = jnp.exp(ldt_ref[...])
    mag = jnp.exp(lr * dt)
    a_re = mag * jnp.cos(li * dt)
    a_im = mag * jnp.sin(li * dt)
    e_re = a_re - 1.0
    den = lr * lr + li * li
    z_re = (e_re * lr + a_im * li) / den
    z_im = (a_im * lr - e_re * li) / den
    are_ref[...] = a_re
    aim_ref[...] = a_im
    br = br_ref[...]
    bi = bi_ref[...]
    bbr_ref[...] = z_re[:, None, :] * br - z_im[:, None, :] * bi
    bbi_ref[...] = z_re[:, None, :] * bi + z_im[:, None, :] * br


def _s5_discretise(lam_re, lam_im, log_dt, b_re_t, b_im_t):
    g, p = lam_re.shape
    return pl.pallas_call(
        _s5_discretise_kernel,
        out_shape=[jax.ShapeDtypeStruct((g, p), F32), jax.ShapeDtypeStruct((g, p), F32),
                   jax.ShapeDtypeStruct(b_re_t.shape, F32), jax.ShapeDtypeStruct(b_re_t.shape, F32)],
        name="s5_discretise",
    )(lam_re, lam_im, log_dt.reshape(g, 1), b_re_t, b_im_t)


def _gelu_tanh(x):
    return 0.5 * x * (1.0 + jnp.tanh(math.sqrt(2.0 / math.pi) * (x + 0.044715 * (x * x * x))))


def _s5_scan_kernel(u_ref, bm_ref, cr_ref, ci_ref, are_ref, aim_ref, d_ref, o_ref,
                    u_sc, ub_sc, bu_ref, xr_ref, xi_ref, y_ref, st_ref, *, ts, nb, half, sub):
    @pl.when(pl.program_id(1) == 0)
    def _():
        st_ref[...] = jnp.zeros_like(st_ref)

    tsub = ts // sub
    rows = tsub * nb
    n_col = 2 * half // MXU_DIM
    n_k = half // MXU_DIM
    u_sc[...] = jnp.swapaxes(u_ref[...], 0, 1).reshape(ts * nb, LANES)
    ub_sc[...] = u_sc[...].astype(BF16)
    a_re = jnp.broadcast_to(are_ref[0], (nb, half))
    a_im = jnp.broadcast_to(aim_ref[0], (nb, half))

    def bu_tile(j, n):
        cols = slice(n * MXU_DIM, (n + 1) * MXU_DIM)
        bu_ref[j % 2, :, cols] = _dot(ub_sc[j * rows:(j + 1) * rows, :], bm_ref[0, :, cols])

    def y_tile(j, n):
        ks = slice((n % n_k) * MXU_DIM, (n % n_k + 1) * MXU_DIM)
        if n < n_k:
            part = _dot(xr_ref[j % 2, :, ks].astype(BF16), cr_ref[0, ks, :])
            y_ref[...] = part if n == 0 else y_ref[...] + part
        else:
            y_ref[...] -= _dot(xi_ref[j % 2, :, ks].astype(BF16), ci_ref[0, ks, :])

    def finish(j):
        y = _gelu_tanh(y_ref[...] + d_ref[0] * u_sc[j * rows:(j + 1) * rows, :])
        o_ref[:, j * tsub:(j + 1) * tsub, :] = jnp.swapaxes(y.reshape(tsub, nb, LANES), 0, 1).astype(o_ref.dtype)

    def scan_piece(j, n, state):
        x_re, x_im = state
        for t in range(n * tsub // n_col, (n + 1) * tsub // n_col):
            r = slice(t * nb, (t + 1) * nb)
            x_re, x_im = (a_re * x_re - a_im * x_im + bu_ref[j % 2, r, 0:half],
                          a_re * x_im + a_im * x_re + bu_ref[j % 2, r, half:2 * half])
            xr_ref[j % 2, r, :] = x_re
            xi_ref[j % 2, r, :] = x_im
        return x_re, x_im

    for n in range(n_col):
        bu_tile(0, n)
    state = (st_ref[:, 0:half], st_ref[:, half:2 * half])
    for j in range(sub):
        for n in range(n_col):
            state = scan_piece(j, n, state)
            if j + 1 < sub:
                bu_tile(j + 1, n)
            if j >= 1:
                y_tile(j - 1, n)
        if j >= 1:
            finish(j - 1)
    for n in range(n_col):
        y_tile(sub - 1, n)
    finish(sub - 1)
    st_ref[:, 0:half] = state[0]
    st_ref[:, half:2 * half] = state[1]


def _s5_scan(u, bmat, cr, ci, a_re, a_im, d_skip, ts=512, sub=4):
    nb, s, d = u.shape
    slabs = d // LANES
    half = bmat.shape[2] // 2
    rows = ts // sub * nb
    io = pl.BlockSpec((nb, ts, LANES), lambda o, i: (0, i, o))
    per_slab = lambda shape: pl.BlockSpec((1,) + shape, lambda o, i: (o, 0, 0))
    return pl.pallas_call(
        functools.partial(_s5_scan_kernel, ts=ts, nb=nb, half=half, sub=sub),
        grid=(slabs, s // ts),
        in_specs=[io, per_slab(bmat.shape[1:]), per_slab(cr.shape[1:]), per_slab(ci.shape[1:]),
                  per_slab((1, half)), per_slab((1, half)), per_slab((1, LANES))],
        out_specs=io,
        out_shape=jax.ShapeDtypeStruct((nb, s, d), BF16),
        scratch_shapes=[pltpu.VMEM((ts * nb, LANES), F32), pltpu.VMEM((ts * nb, LANES), BF16),
                        pltpu.VMEM((2, rows, 2 * half), F32),
                        pltpu.VMEM((2, rows, half), F32), pltpu.VMEM((2, rows, half), F32),
                        pltpu.VMEM((rows, LANES), F32), pltpu.VMEM((nb, 2 * half), F32)],
        compiler_params=_params(("arbitrary", "arbitrary"), 48),
        name="s5_scan",
    )(u, bmat, cr, ci, a_re, a_im, d_skip)


def _block_diag_slabs(t, per_slab):
    g, a, b = t.shape
    t = t.reshape(g // per_slab, per_slab, a, b)
    eye = jnp.eye(per_slab, dtype=t.dtype)
    return jnp.einsum('ogab,gk->ogakb', t, eye).reshape(g // per_slab, per_slab * a, per_slab * b)


def kernel(x, c, norm_mix_g, norm_ffn_g, ada_w, ada_b, ffn_w_in, ffn_w_out, final_norm_g, hy_w_in, hy_w_out, hg_norm_g, hg_lb_logits, s5_w_in, s5_lam_re, s5_lam_im, s5_log_dt, s5_b_re, s5_b_im, s5_c_re, s5_c_im, s5_d, s5_w_glu):
    b, s, d = x.shape
    depth = ada_w.shape[0]
    dff = ffn_w_out.shape[1]
    sb_w = SB_HEADS * SB_HEAD_DIM
    mod = _adaln_mod(c, ada_w, ada_b).reshape(depth, b, N_MOD, d)
    per_slab = LANES // S5_GROUP

    for layer in range(depth):
        i = layer // 2
        wg = ffn_w_in[layer, :, :dff].astype(BF16)
        wu = ffn_w_in[layer, :, dff:].astype(BF16)
        wo = ffn_w_out[layer].astype(BF16)
        gain_m = norm_mix_g[layer].reshape(1, d)
        gain_f = norm_ffn_g[layer].reshape(1, d)
        if layer % 2 == 0:
            qkv, hg = _hybrid_in_proj(x, mod[layer], gain_m, hy_w_in[i].astype(BF16))
            o_a = _sb_attention(qkv)
            o_b = _hgrn2(hg, hg_lb_logits, hg_norm_g[i].reshape(1, -1), layer)
            w_out = hy_w_out[i].astype(BF16)
            x = _hybrid_out_ffn(x, o_a, o_b, mod[layer], gain_f, w_out[:sb_w], w_out[sb_w:], wg, wu, wo)
        else:
            u = _s5_in_proj(x, mod[layer], gain_m, s5_w_in[i].astype(BF16))
            a_re, a_im, bb_re, bb_im = _s5_discretise(
                s5_lam_re[i], s5_lam_im[i], s5_log_dt[i],
                jnp.swapaxes(s5_b_re[i], 1, 2), jnp.swapaxes(s5_b_im[i], 1, 2))
            slabs = d // LANES
            bmat = jnp.concatenate([_block_diag_slabs(bb_re, per_slab), _block_diag_slabs(bb_im, per_slab)],
                                   axis=2).astype(BF16)
            cr = _block_diag_slabs(jnp.swapaxes(s5_c_re[i], 1, 2), per_slab).astype(BF16)
            ci = _block_diag_slabs(jnp.swapaxes(s5_c_im[i], 1, 2), per_slab).astype(BF16)
            half = per_slab * S5_STATE
            y = _s5_scan(u, bmat, cr, ci, a_re.reshape(slabs, 1, half), a_im.reshape(slabs, 1, half),
                         s5_d[i].reshape(slabs, 1, LANES))
            w_glu = s5_w_glu[i].astype(BF16)
            last = layer == depth - 1
            x = _s5_glu_ffn(x, y, mod[layer], gain_f, final_norm_g.reshape(1, d),
                            w_glu[:, :d], w_glu[:, d:], wg, wu, wo)
            assert last, "final norm is fused into the S5 layer's FFN kernel"
    return x
```

```python
import functools
import math

import numpy as np
import jax
import jax.numpy as jnp
from jax import lax
from jax.experimental import pallas as pl
from jax.experimental.pallas import tpu as pltpu

F32 = jnp.float32
BF16 = jnp.bfloat16

EPS = 1e-6
N_MOD = 6
SB_HEADS = 8
SB_HEAD_DIM = 64
HG_HEADS = 4
HG_DIM = 128
HG_CHUNK = 64
S5_GROUP = 16
S5_STATE = 64

LANES = 128
SUBLANES = 8
MXU_DIM = 256
MIB = 1 << 20

SB_DEAD_LOG_WEIGHT = 88.0

NT_DIMS = (((1,), (1,)), ((), ()))
TN_DIMS = (((0,), (0,)), ((), ()))


def _dot(a, b, dims=None):
    if dims is None:
        return jnp.dot(a, b, preferred_element_type=F32)
    return lax.dot_general(a, b, dims, preferred_element_type=F32)


def _split_bf16(x, n):
    parts = []
    rest = x
    for i in range(n):
        p = rest.astype(BF16)
        parts.append(p)
        if i + 1 < n:
            rest = rest - p.astype(F32)
    return parts


def _sigmoid(x):
    return 1.0 / (1.0 + jnp.exp(-x))


def _silu(x):
    return x * _sigmoid(x)


def _rms_scale(x):
    return x * lax.rsqrt(jnp.mean(x * x, axis=-1, keepdims=True) + EPS)


def _norm_mod(x, gain, scale, shift):
    return _rms_scale(x) * (gain * (1.0 + scale)) + shift


def _params(semantics, vmem_mib):
    return pltpu.CompilerParams(dimension_semantics=semantics, vmem_limit_bytes=vmem_mib * MIB)


def _resident(shape, index_map):
    return pl.BlockSpec(shape, index_map, pipeline_mode=pl.Buffered(1))


def _mod_kernel(c_ref, w_ref, b_ref, o_ref):
    a_hi, a_lo = _split_bf16(_silu(c_ref[...]), 2)
    w_hi, w_lo = _split_bf16(w_ref[0], 2)
    acc = _dot(a_hi, w_hi) + _dot(a_hi, w_lo) + _dot(a_lo, w_hi)
    o_ref[0] = acc + b_ref[0]


def _adaln_mod(c, ada_w, ada_b, tn=1536):
    depth, d, n = ada_w.shape
    b = c.shape[0]
    return pl.pallas_call(
        _mod_kernel,
        grid=(depth, n // tn),
        in_specs=[pl.BlockSpec((b, d), lambda l, j: (0, 0)),
                  pl.BlockSpec((1, d, tn), lambda l, j: (l, 0, j)),
                  pl.BlockSpec((1, 1, tn), lambda l, j: (l, 0, j))],
        out_specs=pl.BlockSpec((1, b, tn), lambda l, j: (l, 0, j)),
        out_shape=jax.ShapeDtypeStruct((depth, b, n), F32),
        compiler_params=_params(("arbitrary", "arbitrary"), 40),
        name="adaln_mod",
    )(c, ada_w, ada_b.reshape(depth, 1, n))


def _normmod_proj_kernel(x_ref, mod_ref, gain_ref, w_ref, *o_refs, splits):
    x = x_ref[0]
    h = _norm_mod(x, gain_ref[...], mod_ref[0, 1:2, :], mod_ref[0, 0:1, :]).astype(BF16)
    for o_ref, (c0, c1) in zip(o_refs, splits):
        y = _dot(h, w_ref[:, c0:c1])
        o_ref[...] = y.reshape(o_ref.shape).astype(o_ref.dtype)


def _hybrid_in_proj(x, mod, gain, w, tm=512):
    b, s, d = x.shape
    n = w.shape[1]
    n_a = 3 * SB_HEADS * SB_HEAD_DIM
    return pl.pallas_call(
        functools.partial(_normmod_proj_kernel, splits=((0, n_a), (n_a, n))),
        grid=(b, s // tm),
        in_specs=[pl.BlockSpec((1, tm, d), lambda bi, i: (bi, i, 0)),
                  pl.BlockSpec((1, N_MOD, d), lambda bi, i: (bi, 0, 0)),
                  pl.BlockSpec((1, d), lambda bi, i: (0, 0)),
                  _resident((d, n), lambda bi, i: (0, 0))],
        out_specs=[pl.BlockSpec((1, tm, n_a), lambda bi, i: (bi, i, 0)),
                   pl.BlockSpec((1, tm, n - n_a), lambda bi, i: (bi, i, 0))],
        out_shape=[jax.ShapeDtypeStruct((b, s, n_a), BF16),
                   jax.ShapeDtypeStruct((b, s, n - n_a), F32)],
        compiler_params=_params(("arbitrary", "arbitrary"), 48),
        name="hybrid_in_proj",
    )(x, mod, gain, w)


def _s5_in_proj(x, mod, gain, w, tm=512):
    b, s, d = x.shape
    return pl.pallas_call(
        functools.partial(_normmod_proj_kernel, splits=((0, d),)),
        grid=(b, s // tm),
        in_specs=[pl.BlockSpec((1, tm, d), lambda bi, i: (bi, i, 0)),
                  pl.BlockSpec((1, N_MOD, d), lambda bi, i: (bi, 0, 0)),
                  pl.BlockSpec((1, d), lambda bi, i: (0, 0)),
                  _resident((d, d), lambda bi, i: (0, 0))],
        out_specs=[pl.BlockSpec((1, tm, d), lambda bi, i: (bi, i, 0))],
        out_shape=[jax.ShapeDtypeStruct((b, s, d), F32)],
        compiler_params=_params(("arbitrary", "arbitrary"), 40),
        name="s5_in_proj",
    )(x, mod, gain, w)[0]


SB_BAND = 2 * LANES
SB_TAIL_ROWS = 2 * LANES


def _sb_log_terms(qq, kj, mask, u2, row_sum_on_lanes=False):
    z = _dot(qq, kj, NT_DIMS)
    log_beta = jnp.minimum(z, 0.0) - jnp.log(1.0 + jnp.exp(-jnp.abs(z)))
    log_keep = jnp.where(mask, log_beta - z, 0.0)
    hi, lo = _split_bf16(log_keep, 2)
    sums = _dot(hi, u2) + _dot(lo, u2)
    if not row_sum_on_lanes:
        return log_beta, sums[:, :SB_BAND], sums[:, SB_BAND:]
    ones = jnp.ones((SUBLANES, SB_BAND), BF16)
    return log_beta, sums[:, :SB_BAND], sums[:, SB_BAND:], _dot(ones, hi, NT_DIMS) + _dot(ones, lo, NT_DIMS)


def _sb_weights_dot_v(log_beta, later, carry, mask, vj):
    tail = later if carry is None else later + jnp.concatenate([carry] * (SB_BAND // LANES), axis=1)
    w = jnp.where(mask, jnp.exp(log_beta + tail), 0.0)
    return _dot(w.astype(BF16), vj)


def _sb_band(qq, kj, vj, mask, carry, u2):
    log_beta, later, row_sum = _sb_log_terms(qq, kj, mask, u2)
    pv = _sb_weights_dot_v(log_beta, later, carry, mask, vj)
    return pv, row_sum if carry is None else carry + row_sum


class _SbBlock:
    pass


def _sb_attn_kernel(q_ref, k_ref, v_ref, u2_ref, ur_ref, o_ref, acc_ref, carry_ref, tacc_ref, tcarry_ref,
                    *, tq, dh, scale, pairs, nblk):
    rows = 2 * tq
    cap = SB_TAIL_ROWS
    u2 = u2_ref[...]
    lane = lax.broadcasted_iota(jnp.int32, (tq, LANES), 1)
    row = lax.broadcasted_iota(jnp.int32, (rows, SB_BAND), 0)
    col = lax.broadcasted_iota(jnp.int32, (rows, SB_BAND), 1)
    rel = jnp.where(row >= tq, row - tq, row) - col
    col_c = lax.broadcasted_iota(jnp.int32, (cap, SB_BAND), 1)
    slot = lax.broadcasted_iota(jnp.int32, (cap, rows), 0).astype(F32)
    slot_l = lax.broadcasted_iota(jnp.int32, (cap, LANES), 0).astype(F32)
    ones = jnp.ones((rows, LANES), BF16)
    lanes = [slice(p * LANES, (p + 1) * LANES) for p in range(pairs)]

    def live(state):
        off_prev, cmax = state
        return jnp.logical_and(off_prev > 0, cmax > -SB_DEAD_LOG_WEIGHT)

    def earlier_offset(off_prev):
        return pl.multiple_of(jnp.maximum(off_prev - SB_BAND, 0), LANES)

    def start(blk):
        s = _SbBlock()
        s.blk = blk
        s.q0 = (pl.program_id(1) * nblk + blk) * tq
        s.qqs = []
        for p in range(pairs):
            qs = q_ref[0, blk * tq:(blk + 1) * tq, lanes[p]] * jnp.asarray(scale, q_ref.dtype)
            zero = jnp.zeros_like(qs)
            s.qqs.append(jnp.concatenate([jnp.where(lane < dh, qs, zero), jnp.where(lane >= dh, qs, zero)], axis=0))
        s.off0 = pl.multiple_of(jnp.maximum(s.q0 + tq - SB_BAND, 0), LANES)
        s.causal = rel > (s.off0 - s.q0)
        s.carry_ts, s.carry_max = [], None
        return s

    def first_band(s, p):
        kj = k_ref[0, pl.ds(s.off0, SB_BAND), lanes[p]]
        vj = v_ref[0, pl.ds(s.off0, SB_BAND), lanes[p]]
        log_beta, later, carry, carry_t = _sb_log_terms(s.qqs[p], kj, s.causal, u2, row_sum_on_lanes=True)
        acc_ref[s.blk, p] = _sb_weights_dot_v(log_beta, later, None, s.causal, vj)
        carry_ref[s.blk, p] = carry
        s.carry_ts.append(carry_t)
        s.carry_max = carry if s.carry_max is None else jnp.maximum(s.carry_max, carry)

    def select(s):
        alives = [t > -SB_DEAD_LOG_WEIGHT for t in s.carry_ts]
        alive_bs = [jnp.where(a, 1.0, 0.0).astype(BF16) for a in alives]
        ranks = [_dot(a, ur_ref[...]) for a in alive_bs]
        counts = [_dot(a, ones) for a in alive_bs]
        total = jnp.zeros((SUBLANES, LANES), F32)
        c_c, s.sels = None, []
        for p in range(pairs):
            rank = ranks[p] + jnp.concatenate([total] * (rows // LANES), axis=1)
            hit = jnp.logical_and(rank[0:1, :] == slot, alives[p][0:1, :])
            s.sels.append(jnp.where(hit, 1.0, 0.0).astype(BF16))
            picked = jnp.sum(jnp.where(hit, s.carry_ts[p][0:1, :], 0.0), axis=1, keepdims=True)
            c_c = picked if c_c is None else c_c + picked
            total = total + counts[p]
        s.q_c = jnp.concatenate([_dot(s.sels[p], s.qqs[p]) for p in range(pairs)], axis=1).astype(BF16)
        s.tcarry0 = jnp.where(slot_l < total[0:1, :], jnp.broadcast_to(c_c, (cap, LANES)), -1e30)
        s.fits = jnp.broadcast_to(total[0:1, :], (cap, LANES)) <= cap
        s.cmax0 = jnp.max(s.carry_max)
        s.n_live = jnp.max(total)

    def two_bands_logs(s, off_prev):
        off1 = earlier_offset(off_prev)
        off2 = earlier_offset(off1)
        fresh1 = col_c < (off_prev - off1)
        fresh2 = col_c < (off1 - off2)
        t1 = _sb_log_terms(s.q_c, k_ref[0, pl.ds(off1, SB_BAND), :], fresh1, u2)
        t2 = _sb_log_terms(s.q_c, k_ref[0, pl.ds(off2, SB_BAND), :], fresh2, u2)
        return off1, off2, fresh1, fresh2, t1, t2

    def two_bands_weights(logs, carry0):
        off1, off2, fresh1, fresh2, (lb1, later1, sum1), (lb2, later2, sum2) = logs
        carry1 = carry0 + sum1
        pv1 = _sb_weights_dot_v(lb1, later1, carry0, fresh1, v_ref[0, pl.ds(off1, SB_BAND), :])
        pv2 = _sb_weights_dot_v(lb2, later2, carry1, fresh2, v_ref[0, pl.ds(off2, SB_BAND), :])
        return pv1 + pv2, carry1 + sum2, off2

    def scatter(s, pv):
        for p in range(pairs):
            add = None
            for part in _split_bf16(jnp.where(s.fits, pv[:, lanes[p]], 0.0), 2):
                d = _dot(s.sels[p], part, TN_DIMS)
                add = d if add is None else add + d
            acc_ref[s.blk, p] += add

    def compact_walk(s):
        def logs():
            s.logs = two_bands_logs(s, s.off0)

        def weights():
            s.pv, s.tcarry1, s.off1 = two_bands_weights(s.logs, s.tcarry0)

        def back():
            scatter(s, s.pv)
            s.cmax1 = jnp.max(s.tcarry1)

        return [logs, weights, back]

    blocks, pending = [], []
    for blk in range(nblk):
        s = start(blk)
        for p in range(pairs):
            first_band(s, p)
            if p < len(pending):
                pending[p]()
        select(s)
        blocks.append(s)
        pending = compact_walk(s)
    for part in pending:
        part()

    def walk_live_rows(s):
        tcarry_ref[...] = s.tcarry1
        tacc_ref[...] = jnp.zeros_like(tacc_ref)

        def two_earlier_bands(state):
            off_prev, _ = state
            pv, carry, off = two_bands_weights(two_bands_logs(s, off_prev), tcarry_ref[...])
            tacc_ref[...] += pv
            tcarry_ref[...] = carry
            return off, jnp.max(carry)

        lax.while_loop(live, two_earlier_bands, (s.off1, s.cmax1))
        scatter(s, tacc_ref[...])

    def walk_all_rows(s):
        def earlier_band(state):
            off_prev, _ = state
            off = earlier_offset(off_prev)
            fresh = col < (off_prev - off)
            cmax = None
            for p in range(pairs):
                kj = k_ref[0, pl.ds(off, SB_BAND), lanes[p]]
                vj = v_ref[0, pl.ds(off, SB_BAND), lanes[p]]
                pv, carry = _sb_band(s.qqs[p], kj, vj, fresh, carry_ref[s.blk, p], u2)
                acc_ref[s.blk, p] += pv
                carry_ref[s.blk, p] = carry
                cmax = carry if cmax is None else jnp.maximum(cmax, carry)
            return off, jnp.max(cmax)

        lax.while_loop(live, earlier_band, (s.off0, s.cmax0))

    for s in blocks:
        fits = s.n_live <= cap

        @pl.when(jnp.logical_and(fits, live((s.off1, s.cmax1))))
        def _():
            walk_live_rows(s)

        @pl.when(jnp.logical_and(jnp.logical_not(fits), live((s.off0, s.cmax0))))
        def _():
            walk_all_rows(s)

    for s in blocks:
        for p in range(pairs):
            acc = acc_ref[s.blk, p]
            o_ref[0, s.blk * tq:(s.blk + 1) * tq, lanes[p]] = (
                jnp.where(lane < dh, acc[:tq], acc[tq:]).astype(o_ref.dtype))


def _sb_attention(qkv, tq=LANES, nblk=2):
    b, s, _ = qkv.shape
    width = SB_HEADS * SB_HEAD_DIM
    pairs = width // LANES
    j = np.arange(SB_BAND)
    u2 = np.concatenate([(j[:, None] > j[None, :]).astype(np.float32), np.ones((SB_BAND, LANES), np.float32)],
                        axis=1)
    r = np.arange(2 * tq)
    ur = (r[:, None] < r[None, :]).astype(np.float32)
    return pl.pallas_call(
        functools.partial(_sb_attn_kernel, tq=tq, dh=SB_HEAD_DIM, scale=SB_HEAD_DIM ** -0.5, pairs=pairs,
                          nblk=nblk),
        grid=(b, s // (nblk * tq)),
        in_specs=[pl.BlockSpec((1, nblk * tq, width), lambda bi, i: (bi, i, 0)),
                  pl.BlockSpec((1, s, width), lambda bi, i: (bi, 0, 1)),
                  pl.BlockSpec((1, s, width), lambda bi, i: (bi, 0, 2)),
                  pl.BlockSpec(u2.shape, lambda bi, i: (0, 0)),
                  pl.BlockSpec(ur.shape, lambda bi, i: (0, 0))],
        out_specs=pl.BlockSpec((1, nblk * tq, width), lambda bi, i: (bi, i, 0)),
        out_shape=jax.ShapeDtypeStruct((b, s, width), BF16),
        scratch_shapes=[pltpu.VMEM((nblk, pairs, 2 * tq, LANES), F32), pltpu.VMEM((nblk, pairs, 2 * tq, LANES), F32),
                        pltpu.VMEM((SB_TAIL_ROWS, width), F32), pltpu.VMEM((SB_TAIL_ROWS, LANES), F32)],
        compiler_params=_params(("arbitrary", "arbitrary"), 48),
        name="sb_attention",
    )(qkv, qkv, qkv, jnp.asarray(u2, BF16), jnp.asarray(ur, BF16))


HG_LEVELS = int(math.log2(HG_CHUNK))
HG_DIAG = HG_LEVELS


def _hgrn_tables():
    c = HG_CHUNK
    t = np.arange(c)[:, None]
    j = np.arange(c)[None, :]
    mats = [(j <= t), (j > t)]
    for lv in range(HG_LEVELS):
        m = 1 << lv
        mid = (t // (2 * m)) * (2 * m) + m - 1
        later = (t % (2 * m)) >= m
        mats.append(np.where(later, (j > mid) & (j <= t), (j > t) & (j <= mid)))
    stack = np.concatenate(mats, axis=0).astype(np.float32)
    s = j
    level = np.full((c, c), -1, np.int32)
    for lv in range(HG_LEVELS):
        level[(t > s) & ((t ^ s) >> lv == 1)] = lv
    level[t == s] = HG_DIAG
    return stack, level


def _hgrn_kernel(q_ref, f_ref, i_ref, g_ref, lbl_ref, gain_ref, gs_ref, lvl_ref, o_ref,
                 st_ref, ex_ref, sc_ref, kv_ref, acc_ref, *, layer, nch):
    ch = HG_CHUNK

    @pl.when(pl.program_id(1) == 0)
    def _():
        st_ref[...] = jnp.zeros_like(st_ref)

    logits = lbl_ref[...]
    ex = jnp.exp(logits - jnp.max(logits, axis=0, keepdims=True))
    lb = jnp.sum(ex[:layer + 1], axis=0, keepdims=True) / jnp.sum(ex, axis=0, keepdims=True)
    gs = gs_ref[...]
    lvl = lvl_ref[...]
    tiles = [(c, h, slice(c * ch, (c + 1) * ch), slice(h * HG_DIM, (h + 1) * HG_DIM))
             for c in range(nch) for h in range(HG_HEADS)]

    f = f_ref[0]
    e = jnp.exp(-jnp.abs(f))
    inv = 1.0 / (1.0 + e)
    sig_f = jnp.where(f >= 0, inv, e * inv)
    sig_nf = jnp.where(f >= 0, e * inv, inv)
    log_hi, log_lo = _split_bf16(jnp.log(lb + (1.0 - lb) * sig_f), 2)
    key = (1.0 - lb) * sig_nf
    for c in range(nch):
        rows = slice(c * ch, (c + 1) * ch)
        ex_c = jnp.exp(_dot(gs, log_hi[rows]) + _dot(gs, log_lo[rows]))
        for g in range(2 + HG_LEVELS):
            ex_ref[g, rows, :] = ex_c[g * ch:(g + 1) * ch]
    qa = _silu(q_ref[0])
    vb = i_ref[0].astype(BF16)
    cum_e = ex_ref[0]
    qc = (qa * cum_e).astype(BF16)
    kd = (key * ex_ref[1]).astype(BF16)
    qk = qa * key
    for c, h, rows, hs in tiles:
        sc_ref[c, h] = jnp.where(lvl == HG_DIAG, jnp.sum(qk[rows, hs], axis=1, keepdims=True), 0.0)
        kv_ref[c, h] = _dot(vb[rows, hs], kd[rows, hs], TN_DIMS)
    for lv in range(HG_LEVELS):
        p = ex_ref[2 + lv]
        qe = (qa * p).astype(BF16)
        ke = (key * p).astype(BF16)
        for c, h, rows, hs in tiles:
            s_lv = _dot(qe[rows, hs], ke[rows, hs], NT_DIMS)
            sc_ref[c, h] = jnp.where(lvl == lv, s_lv, sc_ref[c, h])
    for c, h, rows, hs in tiles:
        acc_ref[rows, hs] = _dot(sc_ref[c, h].astype(BF16), vb[rows, hs])
    for c, h, rows, hs in tiles:
        st = st_ref[h]
        kv_c = kv_ref[c, h]
        kv_ref[c, h] = st
        st_ref[h] = st * cum_e[(c + 1) * ch - 1:(c + 1) * ch, hs] + kv_c
    for c, h, rows, hs in tiles:
        acc_ref[rows, hs] += _dot(qc[rows, hs], kv_ref[c, h].astype(BF16), NT_DIMS)
    gate = gain_ref[...] * _silu(g_ref[0])
    for h in range(HG_HEADS):
        hs = slice(h * HG_DIM, (h + 1) * HG_DIM)
        o_ref[0, :, hs] = (_rms_scale(acc_ref[:, hs]) * gate[:, hs]).astype(o_ref.dtype)


def _hgrn2(hg, lb_logits, gain, layer, tb=512):
    b, s, _ = hg.shape
    w = HG_HEADS * HG_DIM
    stack, level = _hgrn_tables()
    blk = lambda k: pl.BlockSpec((1, tb, w), lambda bi, i: (bi, i, k))
    nch = tb // HG_CHUNK
    return pl.pallas_call(
        functools.partial(_hgrn_kernel, layer=layer, nch=nch),
        grid=(b, s // tb),
        in_specs=[blk(0), blk(1), blk(2), blk(3),
                  pl.BlockSpec(lb_logits.shape, lambda bi, i: (0, 0)),
                  pl.BlockSpec((1, w), lambda bi, i: (0, 0)),
                  pl.BlockSpec(stack.shape, lambda bi, i: (0, 0)),
                  pl.BlockSpec(level.shape, lambda bi, i: (0, 0))],
        out_specs=pl.BlockSpec((1, tb, w), lambda bi, i: (bi, i, 0)),
        out_shape=jax.ShapeDtypeStruct((b, s, w), BF16),
        scratch_shapes=[pltpu.VMEM((HG_HEADS, HG_DIM, HG_DIM), F32),
                        pltpu.VMEM((2 + HG_LEVELS, tb, w), F32),
                        pltpu.VMEM((nch, HG_HEADS, HG_CHUNK, HG_CHUNK), F32),
                        pltpu.VMEM((nch, HG_HEADS, HG_DIM, HG_DIM), F32),
                        pltpu.VMEM((tb, w), F32)],
        compiler_params=_params(("arbitrary", "arbitrary"), 40),
        name="hgrn2",
    )(hg, hg, hg, hg, lb_logits, gain, jnp.asarray(stack, BF16), jnp.asarray(level))


def _ffn_chunks(dff, width=512):
    edges = list(range(0, dff, width)) + [dff]
    return tuple(zip(edges[:-1], edges[1:]))


def _ffn_residual(x1, mod_ref, gain_ref, wg_ref, wu_ref, wo_ref, a_ref):
    h = _norm_mod(x1, gain_ref[...], mod_ref[0, 4:5, :], mod_ref[0, 3:4, :]).astype(BF16)
    for c0, c1 in _ffn_chunks(wg_ref.shape[1]):
        g = _dot(h, wg_ref[:, c0:c1])
        u = _dot(h, wu_ref[:, c0:c1])
        a_ref[:, c0:c1] = (_silu(g) * u).astype(BF16)
    return x1 + mod_ref[0, 5:6, :] * _dot(a_ref[...], wo_ref[...])


def _hybrid_out_ffn_kernel(x_ref, oa_ref, ob_ref, mod_ref, gain_ref, wa_ref, wb_ref, wg_ref, wu_ref, wo_ref,
                           o_ref, a_ref):
    mix = _dot(oa_ref[0], wa_ref[...]) + _dot(ob_ref[0], wb_ref[...])
    x1 = x_ref[0] + mod_ref[0, 2:3, :] * mix
    o_ref[0] = _ffn_residual(x1, mod_ref, gain_ref, wg_ref, wu_ref, wo_ref, a_ref)


def _hybrid_out_ffn(x, o_a, o_b, mod, gain_f, w_a, w_b, wg, wu, wo, tm=512):
    b, s, d = x.shape
    dff = wg.shape[1]
    row = lambda n: pl.BlockSpec((1, tm, n), lambda bi, i: (bi, i, 0))
    const = lambda shape: _resident(shape, lambda bi, i: (0, 0))
    return pl.pallas_call(
        _hybrid_out_ffn_kernel,
        grid=(b, s // tm),
        in_specs=[row(d), row(o_a.shape[2]), row(o_b.shape[2]),
                  pl.BlockSpec((1, N_MOD, d), lambda bi, i: (bi, 0, 0)),
                  pl.BlockSpec((1, d), lambda bi, i: (0, 0)),
                  const(w_a.shape), const(w_b.shape), const(wg.shape), const(wu.shape), const(wo.shape)],
        out_specs=row(d),
        out_shape=jax.ShapeDtypeStruct((b, s, d), F32),
        scratch_shapes=[pltpu.VMEM((tm, dff), BF16)],
        compiler_params=_params(("arbitrary", "arbitrary"), 56),
        name="hybrid_out_ffn",
    )(x, o_a, o_b, mod, gain_f, w_a, w_b, wg, wu, wo)


def _s5_glu_ffn_kernel(x_ref, y_ref, mod_ref, gain_ref, fin_ref, wv_ref, wt_ref, wg_ref, wu_ref, wo_ref,
                       o_ref, a_ref):
    y = y_ref[0]
    mix = _dot(y, wv_ref[...]) * _sigmoid(_dot(y, wt_ref[...]))
    x1 = x_ref[0] + mod_ref[0, 2:3, :] * mix
    x2 = _ffn_residual(x1, mod_ref, gain_ref, wg_ref, wu_ref, wo_ref, a_ref)
    o_ref[0] = _rms_scale(x2) * fin_ref[...]


def _s5_glu_ffn(x, y, mod, gain_f, final_g, wv, wt, wg, wu, wo, tm=512):
    b, s, d = x.shape
    dff = wg.shape[1]
    const = lambda shape: _resident(shape, lambda bi, i: (0, 0))
    return pl.pallas_call(
        _s5_glu_ffn_kernel,
        grid=(b, s // tm),
        in_specs=[pl.BlockSpec((1, tm, d), lambda bi, i: (bi, i, 0)),
                  pl.BlockSpec((1, tm, d), lambda bi, i: (bi, i, 0)),
                  pl.BlockSpec((1, N_MOD, d), lambda bi, i: (bi, 0, 0)),
                  pl.BlockSpec((1, d), lambda bi, i: (0, 0)),
                  pl.BlockSpec((1, d), lambda bi, i: (0, 0)),
                  const(wv.shape), const(wt.shape), const(wg.shape), const(wu.shape), const(wo.shape)],
        out_specs=pl.BlockSpec((1, tm, d), lambda bi, i: (bi, i, 0)),
        out_shape=jax.ShapeDtypeStruct((b, s, d), F32),
        scratch_shapes=[pltpu.VMEM((tm, dff), BF16)],
        compiler_params=_params(("arbitrary", "arbitrary"), 56),
        name="s5_glu_ffn",
    )(x, y, mod, gain_f, final_g, wv, wt, wg, wu, wo)


def _s5_discretise_kernel(lr_ref, li_ref, ldt_ref, br_ref, bi_ref, are_ref, aim_ref, bbr_ref, bbi_ref):
    lr = jnp.minimum(lr_ref[...], -1e-4)
    li = li_ref[...]
    dt = jnp.exp(ldt_ref[...])
    mag = jnp.exp(lr * dt)
    a_re = mag * jnp.cos(li * dt)
    a_im = mag * jnp.sin(li * dt)
    e_re = a_re - 1.0
    den = lr * lr + li * li
    z_re = (e_re * lr + a_im * li) / den
    z_im = (a_im * lr - e_re * li) / den
    are_ref[...] = a_re
    aim_ref[...] = a_im
    br = br_ref[...]
    bi = bi_ref[...]
    bbr_ref[...] = z_re[:, None, :] * br - z_im[:, None, :] * bi
    bbi_ref[...] = z_re[:, None, :] * bi + z_im[:, None, :] * br


def _s5_discretise(lam_re, lam_im, log_dt, b_re_t, b_im_t):
    g, p = lam_re.shape
    return pl.pallas_call(
        _s5_discretise_kernel,
        out_shape=[jax.ShapeDtypeStruct((g, p), F32), jax.ShapeDtypeStruct((g, p), F32),
                   jax.ShapeDtypeStruct(b_re_t.shape, F32), jax.ShapeDtypeStruct(b_re_t.shape, F32)],
        name="s5_discretise",
    )(lam_re, lam_im, log_dt.reshape(g, 1), b_re_t, b_im_t)


def _gelu_tanh(x):
    return 0.5 * x * (1.0 + jnp.tanh(math.sqrt(2.0 / math.pi) * (x + 0.044715 * (x * x * x))))


def _s5_scan_kernel(u_ref, bm_ref, cr_ref, ci_ref, are_ref, aim_ref, d_ref, o_ref,
                    u_sc, ub_sc, bu_ref, xr_ref, xi_ref, y_ref, st_ref, *, ts, nb, half, sub):
    @pl.when(pl.program_id(1) == 0)
    def _():
        st_ref[...] = jnp.zeros_like(st_ref)

    tsub = ts // sub
    rows = tsub * nb
    n_col = 2 * half // MXU_DIM
    n_k = half // MXU_DIM
    u_sc[...] = jnp.swapaxes(u_ref[...], 0, 1).reshape(ts * nb, LANES)
    ub_sc[...] = u_sc[...].astype(BF16)
    a_re = jnp.broadcast_to(are_ref[0], (nb, half))
    a_im = jnp.broadcast_to(aim_ref[0], (nb, half))

    def bu_tile(j, n):
        cols = slice(n * MXU_DIM, (n + 1) * MXU_DIM)
        bu_ref[j % 2, :, cols] = _dot(ub_sc[j * rows:(j + 1) * rows, :], bm_ref[0, :, cols])

    def y_tile(j, n):
        ks = slice((n % n_k) * MXU_DIM, (n % n_k + 1) * MXU_DIM)
        if n < n_k:
            part = _dot(xr_ref[j % 2, :, ks].astype(BF16), cr_ref[0, ks, :])
            y_ref[...] = part if n == 0 else y_ref[...] + part
        else:
            y_ref[...] -= _dot(xi_ref[j % 2, :, ks].astype(BF16), ci_ref[0, ks, :])

    def finish(j):
        y = _gelu_tanh(y_ref[...] + d_ref[0] * u_sc[j * rows:(j + 1) * rows, :])
        o_ref[:, j * tsub:(j + 1) * tsub, :] = jnp.swapaxes(y.reshape(tsub, nb, LANES), 0, 1).astype(o_ref.dtype)

    def scan_piece(j, n, state):
        x_re, x_im = state
        for t in range(n * tsub // n_col, (n + 1) * tsub // n_col):
            r = slice(t * nb, (t + 1) * nb)
            x_re, x_im = (a_re * x_re - a_im * x_im + bu_ref[j % 2, r, 0:half],
                          a_re * x_im + a_im * x_re + bu_ref[j % 2, r, half:2 * half])
            xr_ref[j % 2, r, :] = x_re
            xi_ref[j % 2, r, :] = x_im
        return x_re, x_im

    for n in range(n_col):
        bu_tile(0, n)
    state = (st_ref[:, 0:half], st_ref[:, half:2 * half])
    for j in range(sub):
        for n in range(n_col):
            state = scan_piece(j, n, state)
            if j + 1 < sub:
                bu_tile(j + 1, n)
            if j >= 1:
                y_tile(j - 1, n)
        if j >= 1:
            finish(j - 1)
    for n in range(n_col):
        y_tile(sub - 1, n)
    finish(sub - 1)
    st_ref[:, 0:half] = state[0]
    st_ref[:, half:2 * half] = state[1]


def _s5_scan(u, bmat, cr, ci, a_re, a_im, d_skip, ts=512, sub=4):
    nb, s, d = u.shape
    slabs = d // LANES
    half = bmat.shape[2] // 2
    rows = ts // sub * nb
    io = pl.BlockSpec((nb, ts, LANES), lambda o, i: (0, i, o))
    per_slab = lambda shape: pl.BlockSpec((1,) + shape, lambda o, i: (o, 0, 0))
    return pl.pallas_call(
        functools.partial(_s5_scan_kernel, ts=ts, nb=nb, half=half, sub=sub),
        grid=(slabs, s // ts),
        in_specs=[io, per_slab(bmat.shape[1:]), per_slab(cr.shape[1:]), per_slab(ci.shape[1:]),
                  per_slab((1, half)), per_slab((1, half)), per_slab((1, LANES))],
        out_specs=io,
        out_shape=jax.ShapeDtypeStruct((nb, s, d), BF16),
        scratch_shapes=[pltpu.VMEM((ts * nb, LANES), F32), pltpu.VMEM((ts * nb, LANES), BF16),
                        pltpu.VMEM((2, rows, 2 * half), F32),
                        pltpu.VMEM((2, rows, half), F32), pltpu.VMEM((2, rows, half), F32),
                        pltpu.VMEM((rows, LANES), F32), pltpu.VMEM((nb, 2 * half), F32)],
        compiler_params=_params(("arbitrary", "arbitrary"), 48),
        name="s5_scan",
    )(u, bmat, cr, ci, a_re, a_im, d_skip)


def _block_diag_slabs(t, per_slab):
    g, a, b = t.shape
    t = t.reshape(g // per_slab, per_slab, a, b)
    eye = jnp.eye(per_slab, dtype=t.dtype)
    return jnp.einsum('ogab,gk->ogakb', t, eye).reshape(g // per_slab, per_slab * a, per_slab * b)


def kernel(x, c, norm_mix_g, norm_ffn_g, ada_w, ada_b, ffn_w_in, ffn_w_out, final_norm_g, hy_w_in, hy_w_out, hg_norm_g, hg_lb_logits, s5_w_in, s5_lam_re, s5_lam_im, s5_log_dt, s5_b_re, s5_b_im, s5_c_re, s5_c_im, s5_d, s5_w_glu):
    b, s, d = x.shape
    depth = ada_w.shape[0]
    dff = ffn_w_out.shape[1]
    sb_w = SB_HEADS * SB_HEAD_DIM
    mod = _adaln_mod(c, ada_w, ada_b).reshape(depth, b, N_MOD, d)
    per_slab = LANES // S5_GROUP

    for layer in range(depth):
        i = layer // 2
        wg = ffn_w_in[layer, :, :dff].astype(BF16)
        wu = ffn_w_in[layer, :, dff:].astype(BF16)
        wo = ffn_w_out[layer].astype(BF16)
        gain_m = norm_mix_g[layer].reshape(1, d)
        gain_f = norm_ffn_g[layer].reshape(1, d)
        if layer % 2 == 0:
            qkv, hg = _hybrid_in_proj(x, mod[layer], gain_m, hy_w_in[i].astype(BF16))
            o_a = _sb_attention(qkv)
            o_b = _hgrn2(hg, hg_lb_logits, hg_norm_g[i].reshape(1, -1), layer)
            w_out = hy_w_out[i].astype(BF16)
            x = _hybrid_out_ffn(x, o_a, o_b, mod[layer], gain_f, w_out[:sb_w], w_out[sb_w:], wg, wu, wo)
        else:
            u = _s5_in_proj(x, mod[layer], gain_m, s5_w_in[i].astype(BF16))
            a_re, a_im, bb_re, bb_im = _s5_discretise(
                s5_lam_re[i], s5_lam_im[i], s5_log_dt[i],
                jnp.swapaxes(s5_b_re[i], 1, 2), jnp.swapaxes(s5_b_im[i], 1, 2))
            slabs = d // LANES
            bmat = jnp.concatenate([_block_diag_slabs(bb_re, per_slab), _block_diag_slabs(bb_im, per_slab)],
                                   axis=2).astype(BF16)
            cr = _block_diag_slabs(jnp.swapaxes(s5_c_re[i], 1, 2), per_slab).astype(BF16)
            ci = _block_diag_slabs(jnp.swapaxes(s5_c_im[i], 1, 2), per_slab).astype(BF16)
            half = per_slab * S5_STATE
            y = _s5_scan(u, bmat, cr, ci, a_re.reshape(slabs, 1, half), a_im.reshape(slabs, 1, half),
                         s5_d[i].reshape(slabs, 1, LANES))
            w_glu = s5_w_glu[i].astype(BF16)
            last = layer == depth - 1
            x = _s5_glu_ffn(x, y, mod[layer], gain_f, final_norm_g.reshape(1, d),
                            w_glu[:, :d], w_glu[:, d:], wg, wu, wo)
            assert last, "final norm is fused into the S5 layer's FFN kernel"
    return x
```

```python
import functools
import math

import numpy as np
import jax
import jax.numpy as jnp
from jax import lax
from jax.experimental import pallas as pl
from jax.experimental.pallas import tpu as pltpu

F32 = jnp.float32
BF16 = jnp.bfloat16

EPS = 1e-6
N_MOD = 6
SB_HEADS = 8
SB_HEAD_DIM = 64
HG_HEADS = 4
HG_DIM = 128
HG_CHUNK = 64
S5_GROUP = 16
S5_STATE = 64

LANES = 128
SUBLANES = 8
MXU_DIM = 256
MIB = 1 << 20

SB_DEAD_LOG_WEIGHT = 88.0

NT_DIMS = (((1,), (1,)), ((), ()))
TN_DIMS = (((0,), (0,)), ((), ()))


def _dot(a, b, dims=None):
    if dims is None:
        return jnp.dot(a, b, preferred_element_type=F32)
    return lax.dot_general(a, b, dims, preferred_element_type=F32)


def _split_bf16(x, n):
    parts = []
    rest = x
    for i in range(n):
        p = rest.astype(BF16)
        parts.append(p)
        if i + 1 < n:
            rest = rest - p.astype(F32)
    return parts


def _sigmoid(x):
    return 1.0 / (1.0 + jnp.exp(-x))


def _silu(x):
    return x * _sigmoid(x)


def _rms_scale(x):
    return x * lax.rsqrt(jnp.mean(x * x, axis=-1, keepdims=True) + EPS)


def _norm_mod(x, gain, scale, shift):
    return _rms_scale(x) * (gain * (1.0 + scale)) + shift


def _params(semantics, vmem_mib):
    return pltpu.CompilerParams(dimension_semantics=semantics, vmem_limit_bytes=vmem_mib * MIB)


def _resident(shape, index_map):
    return pl.BlockSpec(shape, index_map, pipeline_mode=pl.Buffered(1))


def _mod_kernel(c_ref, w_ref, b_ref, o_ref):
    a_hi, a_lo = _split_bf16(_silu(c_ref[...]), 2)
    w_hi, w_lo = _split_bf16(w_ref[0], 2)
    acc = _dot(a_hi, w_hi) + _dot(a_hi, w_lo) + _dot(a_lo, w_hi)
    o_ref[0] = acc + b_ref[0]


def _adaln_mod(c, ada_w, ada_b, tn=1536):
    depth, d, n = ada_w.shape
    b = c.shape[0]
    return pl.pallas_call(
        _mod_kernel,
        grid=(depth, n // tn),
        in_specs=[pl.BlockSpec((b, d), lambda l, j: (0, 0)),
                  pl.BlockSpec((1, d, tn), lambda l, j: (l, 0, j)),
                  pl.BlockSpec((1, 1, tn), lambda l, j: (l, 0, j))],
        out_specs=pl.BlockSpec((1, b, tn), lambda l, j: (l, 0, j)),
        out_shape=jax.ShapeDtypeStruct((depth, b, n), F32),
        compiler_params=_params(("arbitrary", "arbitrary"), 40),
        name="adaln_mod",
    )(c, ada_w, ada_b.reshape(depth, 1, n))


def _normmod_proj_kernel(x_ref, mod_ref, gain_ref, w_ref, *o_refs, splits):
    x = x_ref[0]
    h = _norm_mod(x, gain_ref[...], mod_ref[0, 1:2, :], mod_ref[0, 0:1, :]).astype(BF16)
    for o_ref, (c0, c1) in zip(o_refs, splits):
        y = _dot(h, w_ref[:, c0:c1])
        o_ref[...] = y.reshape(o_ref.shape).astype(o_ref.dtype)


def _hybrid_in_proj(x, mod, gain, w, tm=512):
    b, s, d = x.shape
    n = w.shape[1]
    n_a = 3 * SB_HEADS * SB_HEAD_DIM
    return pl.pallas_call(
        functools.partial(_normmod_proj_kernel, splits=((0, n_a), (n_a, n))),
        grid=(b, s // tm),
        in_specs=[pl.BlockSpec((1, tm, d), lambda bi, i: (bi, i, 0)),
                  pl.BlockSpec((1, N_MOD, d), lambda bi, i: (bi, 0, 0)),
                  pl.BlockSpec((1, d), lambda bi, i: (0, 0)),
                  _resident((d, n), lambda bi, i: (0, 0))],
        out_specs=[pl.BlockSpec((1, tm, n_a), lambda bi, i: (bi, i, 0)),
                   pl.BlockSpec((1, tm, n - n_a), lambda bi, i: (bi, i, 0))],
        out_shape=[jax.ShapeDtypeStruct((b, s, n_a), BF16),
                   jax.ShapeDtypeStruct((b, s, n - n_a), F32)],
        compiler_params=_params(("arbitrary", "arbitrary"), 48),
        name="hybrid_in_proj",
    )(x, mod, gain, w)


def _s5_in_proj(x, mod, gain, w, tm=512):
    b, s, d = x.shape
    return pl.pallas_call(
        functools.partial(_normmod_proj_kernel, splits=((0, d),)),
        grid=(b, s // tm),
        in_specs=[pl.BlockSpec((1, tm, d), lambda bi, i: (bi, i, 0)),
                  pl.BlockSpec((1, N_MOD, d), lambda bi, i: (bi, 0, 0)),
                  pl.BlockSpec((1, d), lambda bi, i: (0, 0)),
                  _resident((d, d), lambda bi, i: (0, 0))],
        out_specs=[pl.BlockSpec((1, tm, d), lambda bi, i: (bi, i, 0))],
        out_shape=[jax.ShapeDtypeStruct((b, s, d), F32)],
        compiler_params=_params(("arbitrary", "arbitrary"), 40),
        name="s5_in_proj",
    )(x, mod, gain, w)[0]


SB_BAND = 2 * LANES
SB_TAIL_ROWS = 2 * LANES


def _sb_log_terms(qq, kj, mask, u2, row_sum_on_lanes=False):
    z = _dot(qq, kj, NT_DIMS)
    log_beta = jnp.minimum(z, 0.0) - jnp.log(1.0 + jnp.exp(-jnp.abs(z)))
    log_keep = jnp.where(mask, log_beta - z, 0.0)
    hi, lo = _split_bf16(log_keep, 2)
    later = _dot(hi, u2) + _dot(lo, u2)
    row_sum = jnp.sum(log_keep, axis=1, keepdims=True)
    if not row_sum_on_lanes:
        return log_beta, later, row_sum
    ones = jnp.ones((SUBLANES, SB_BAND), BF16)
    return log_beta, later, row_sum, _dot(ones, hi, NT_DIMS) + _dot(ones, lo, NT_DIMS)


def _sb_weights_dot_v(log_beta, later, carry, mask, vj):
    tail = later if carry is None else later + carry
    w = jnp.where(mask, jnp.exp(log_beta + tail), 0.0)
    return _dot(w.astype(BF16), vj)


def _sb_band(qq, kj, vj, mask, carry, u2):
    log_beta, later, row_sum = _sb_log_terms(qq, kj, mask, u2)
    pv = _sb_weights_dot_v(log_beta, later, carry, mask, vj)
    return pv, row_sum if carry is None else carry + row_sum


class _SbBlock:
    pass


def _sb_attn_kernel(q_ref, k_ref, v_ref, u2_ref, ur_ref, o_ref, acc_ref, carry_ref, tacc_ref, tcarry_ref,
                    *, tq, dh, scale, pairs, nblk):
    rows = 2 * tq
    cap = SB_TAIL_ROWS
    u2 = u2_ref[...]
    lane = lax.broadcasted_iota(jnp.int32, (tq, LANES), 1)
    row = lax.broadcasted_iota(jnp.int32, (rows, SB_BAND), 0)
    col = lax.broadcasted_iota(jnp.int32, (rows, SB_BAND), 1)
    rel = jnp.where(row >= tq, row - tq, row) - col
    col_c = lax.broadcasted_iota(jnp.int32, (cap, SB_BAND), 1)
    slot = lax.broadcasted_iota(jnp.int32, (cap, rows), 0).astype(F32)
    slot_l = lax.broadcasted_iota(jnp.int32, (cap, LANES), 0).astype(F32)
    ones = jnp.ones((rows, LANES), BF16)
    lanes = [slice(p * LANES, (p + 1) * LANES) for p in range(pairs)]

    def live(state):
        off_prev, cmax = state
        return jnp.logical_and(off_prev > 0, cmax > -SB_DEAD_LOG_WEIGHT)

    def earlier_offset(off_prev):
        return pl.multiple_of(jnp.maximum(off_prev - SB_BAND, 0), LANES)

    def start(blk):
        s = _SbBlock()
        s.blk = blk
        s.q0 = (pl.program_id(1) * nblk + blk) * tq
        s.qqs = []
        for p in range(pairs):
            qs = q_ref[0, blk * tq:(blk + 1) * tq, lanes[p]] * jnp.asarray(scale, q_ref.dtype)
            zero = jnp.zeros_like(qs)
            s.qqs.append(jnp.concatenate([jnp.where(lane < dh, qs, zero), jnp.where(lane >= dh, qs, zero)], axis=0))
        s.off0 = pl.multiple_of(jnp.maximum(s.q0 + tq - SB_BAND, 0), LANES)
        s.causal = rel > (s.off0 - s.q0)
        s.carry_ts, s.carry_max = [], None
        return s

    def first_band(s, p):
        kj = k_ref[0, pl.ds(s.off0, SB_BAND), lanes[p]]
        vj = v_ref[0, pl.ds(s.off0, SB_BAND), lanes[p]]
        log_beta, later, carry, carry_t = _sb_log_terms(s.qqs[p], kj, s.causal, u2, row_sum_on_lanes=True)
        acc_ref[s.blk, p] = _sb_weights_dot_v(log_beta, later, None, s.causal, vj)
        carry_ref[s.blk, p] = carry
        s.carry_ts.append(carry_t)
        s.carry_max = carry if s.carry_max is None else jnp.maximum(s.carry_max, carry)

    def select(s):
        alives = [t > -SB_DEAD_LOG_WEIGHT for t in s.carry_ts]
        alive_bs = [jnp.where(a, 1.0, 0.0).astype(BF16) for a in alives]
        ranks = [_dot(a, ur_ref[...]) for a in alive_bs]
        counts = [_dot(a, ones) for a in alive_bs]
        total = jnp.zeros((SUBLANES, LANES), F32)
        c_c, s.sels = None, []
        for p in range(pairs):
            rank = ranks[p] + jnp.concatenate([total] * (rows // LANES), axis=1)
            hit = jnp.logical_and(rank[0:1, :] == slot, alives[p][0:1, :])
            s.sels.append(jnp.where(hit, 1.0, 0.0).astype(BF16))
            picked = jnp.sum(jnp.where(hit, s.carry_ts[p][0:1, :], 0.0), axis=1, keepdims=True)
            c_c = picked if c_c is None else c_c + picked
            total = total + counts[p]
        s.q_c = jnp.concatenate([_dot(s.sels[p], s.qqs[p]) for p in range(pairs)], axis=1).astype(BF16)
        s.tcarry0 = jnp.where(slot_l[:, 0:1] < total[0:1, 0:1], c_c, -1e30)
        s.fits = jnp.broadcast_to(total[0:1, :], (cap, LANES)) <= cap
        s.cmax0 = jnp.max(s.carry_max)
        s.n_live = jnp.max(total)

    def two_bands_logs(s, off_prev):
        off1 = earlier_offset(off_prev)
        off2 = earlier_offset(off1)
        fresh1 = col_c < (off_prev - off1)
        fresh2 = col_c < (off1 - off2)
        t1 = _sb_log_terms(s.q_c, k_ref[0, pl.ds(off1, SB_BAND), :], fresh1, u2)
        t2 = _sb_log_terms(s.q_c, k_ref[0, pl.ds(off2, SB_BAND), :], fresh2, u2)
        return off1, off2, fresh1, fresh2, t1, t2

    def two_bands_weights(logs, carry0):
        off1, off2, fresh1, fresh2, (lb1, later1, sum1), (lb2, later2, sum2) = logs
        carry1 = carry0 + sum1
        pv1 = _sb_weights_dot_v(lb1, later1, carry0, fresh1, v_ref[0, pl.ds(off1, SB_BAND), :])
        pv2 = _sb_weights_dot_v(lb2, later2, carry1, fresh2, v_ref[0, pl.ds(off2, SB_BAND), :])
        return pv1 + pv2, carry1 + sum2, off2

    def scatter(s, pv):
        for p in range(pairs):
            add = None
            for part in _split_bf16(jnp.where(s.fits, pv[:, lanes[p]], 0.0), 2):
                d = _dot(s.sels[p], part, TN_DIMS)
                add = d if add is None else add + d
            acc_ref[s.blk, p] += add

    def compact_walk(s):
        def logs():
            s.logs = two_bands_logs(s, s.off0)

        def weights():
            s.pv, s.tcarry1, s.off1 = two_bands_weights(s.logs, s.tcarry0)

        def back():
            scatter(s, s.pv)
            s.cmax1 = jnp.max(s.tcarry1)

        return [logs, weights, back]

    blocks, pending = [], []
    for blk in range(nblk):
        s = start(blk)
        for p in range(pairs):
            first_band(s, p)
            if p < len(pending):
                pending[p]()
        select(s)
        blocks.append(s)
        pending = compact_walk(s)
    for part in pending:
        part()

    def walk_live_rows(s):
        tcarry_ref[...] = s.tcarry1
        tacc_ref[...] = jnp.zeros_like(tacc_ref)

        def two_earlier_bands(state):
            off_prev, _ = state
            pv, carry, off = two_bands_weights(two_bands_logs(s, off_prev), tcarry_ref[...])
            tacc_ref[...] += pv
            tcarry_ref[...] = carry
            return off, jnp.max(carry)

        lax.while_loop(live, two_earlier_bands, (s.off1, s.cmax1))
        scatter(s, tacc_ref[...])

    def walk_all_rows(s):
        def earlier_band(state):
            off_prev, _ = state
            off = earlier_offset(off_prev)
            fresh = col < (off_prev - off)
            cmax = None
            for p in range(pairs):
                kj = k_ref[0, pl.ds(off, SB_BAND), lanes[p]]
                vj = v_ref[0, pl.ds(off, SB_BAND), lanes[p]]
                pv, carry = _sb_band(s.qqs[p], kj, vj, fresh, carry_ref[s.blk, p], u2)
                acc_ref[s.blk, p] += pv
                carry_ref[s.blk, p] = carry
                cmax = carry if cmax is None else jnp.maximum(cmax, carry)
            return off, jnp.max(cmax)

        lax.while_loop(live, earlier_band, (s.off0, s.cmax0))

    for s in blocks:
        fits = s.n_live <= cap

        @pl.when(jnp.logical_and(fits, live((s.off1, s.cmax1))))
        def _():
            walk_live_rows(s)

        @pl.when(jnp.logical_and(jnp.logical_not(fits), live((s.off0, s.cmax0))))
        def _():
            walk_all_rows(s)

    for s in blocks:
        for p in range(pairs):
            acc = acc_ref[s.blk, p]
            o_ref[0, s.blk * tq:(s.blk + 1) * tq, lanes[p]] = (
                jnp.where(lane < dh, acc[:tq], acc[tq:]).astype(o_ref.dtype))


def _sb_attention(qkv, tq=LANES, nblk=2):
    b, s, _ = qkv.shape
    width = SB_HEADS * SB_HEAD_DIM
    pairs = width // LANES
    j = np.arange(SB_BAND)
    u2 = (j[:, None] > j[None, :]).astype(np.float32)
    r = np.arange(2 * tq)
    ur = (r[:, None] < r[None, :]).astype(np.float32)
    return pl.pallas_call(
        functools.partial(_sb_attn_kernel, tq=tq, dh=SB_HEAD_DIM, scale=SB_HEAD_DIM ** -0.5, pairs=pairs,
                          nblk=nblk),
        grid=(b, s // (nblk * tq)),
        in_specs=[pl.BlockSpec((1, nblk * tq, width), lambda bi, i: (bi, i, 0)),
                  pl.BlockSpec((1, s, width), lambda bi, i: (bi, 0, 1)),
                  pl.BlockSpec((1, s, width), lambda bi, i: (bi, 0, 2)),
                  pl.BlockSpec(u2.shape, lambda bi, i: (0, 0)),
                  pl.BlockSpec(ur.shape, lambda bi, i: (0, 0))],
        out_specs=pl.BlockSpec((1, nblk * tq, width), lambda bi, i: (bi, i, 0)),
        out_shape=jax.ShapeDtypeStruct((b, s, width), BF16),
        scratch_shapes=[pltpu.VMEM((nblk, pairs, 2 * tq, LANES), F32), pltpu.VMEM((nblk, pairs, 2 * tq, 1), F32),
                        pltpu.VMEM((SB_TAIL_ROWS, width), F32), pltpu.VMEM((SB_TAIL_ROWS, 1), F32)],
        compiler_params=_params(("arbitrary", "arbitrary"), 48),
        name="sb_attention",
    )(qkv, qkv, qkv, jnp.asarray(u2, BF16), jnp.asarray(ur, BF16))


HG_LEVELS = int(math.log2(HG_CHUNK))
HG_DIAG = HG_LEVELS


def _hgrn_tables():
    c = HG_CHUNK
    t = np.arange(c)[:, None]
    j = np.arange(c)[None, :]
    mats = [(j <= t), (j > t)]
    for lv in range(HG_LEVELS):
        m = 1 << lv
        mid = (t // (2 * m)) * (2 * m) + m - 1
        later = (t % (2 * m)) >= m
        mats.append(np.where(later, (j > mid) & (j <= t), (j > t) & (j <= mid)))
    stack = np.concatenate(mats, axis=0).astype(np.float32)
    s = j
    level = np.full((c, c), -1, np.int32)
    for lv in range(HG_LEVELS):
        level[(t > s) & ((t ^ s) >> lv == 1)] = lv
    level[t == s] = HG_DIAG
    return stack, level


def _hgrn_kernel(q_ref, f_ref, i_ref, g_ref, lbl_ref, gain_ref, gs_ref, lvl_ref, o_ref,
                 st_ref, ex_ref, sc_ref, kv_ref, acc_ref, *, layer, nch):
    ch = HG_CHUNK

    @pl.when(pl.program_id(1) == 0)
    def _():
        st_ref[...] = jnp.zeros_like(st_ref)

    logits = lbl_ref[...]
    ex = jnp.exp(logits - jnp.max(logits, axis=0, keepdims=True))
    lb = jnp.sum(ex[:layer + 1], axis=0, keepdims=True) / jnp.sum(ex, axis=0, keepdims=True)
    gs = gs_ref[...]
    lvl = lvl_ref[...]
    tiles = [(c, h, slice(c * ch, (c + 1) * ch), slice(h * HG_DIM, (h + 1) * HG_DIM))
             for c in range(nch) for h in range(HG_HEADS)]

    f = f_ref[0]
    e = jnp.exp(-jnp.abs(f))
    inv = 1.0 / (1.0 + e)
    sig_f = jnp.where(f >= 0, inv, e * inv)
    sig_nf = jnp.where(f >= 0, e * inv, inv)
    log_hi, log_lo = _split_bf16(jnp.log(lb + (1.0 - lb) * sig_f), 2)
    key = (1.0 - lb) * sig_nf
    for c in range(nch):
        rows = slice(c * ch, (c + 1) * ch)
        ex_c = jnp.exp(_dot(gs, log_hi[rows]) + _dot(gs, log_lo[rows]))
        for g in range(2 + HG_LEVELS):
            ex_ref[g, rows, :] = ex_c[g * ch:(g + 1) * ch]
    qa = _silu(q_ref[0])
    vb = i_ref[0].astype(BF16)
    cum_e = ex_ref[0]
    qc = (qa * cum_e).astype(BF16)
    kd = (key * ex_ref[1]).astype(BF16)
    qk = qa * key
    for c, h, rows, hs in tiles:
        sc_ref[c, h] = jnp.where(lvl == HG_DIAG, jnp.sum(qk[rows, hs], axis=1, keepdims=True), 0.0)
        kv_ref[c, h] = _dot(vb[rows, hs], kd[rows, hs], TN_DIMS)
    for lv in range(HG_LEVELS):
        p = ex_ref[2 + lv]
        qe = (qa * p).astype(BF16)
        ke = (key * p).astype(BF16)
        for c, h, rows, hs in tiles:
            s_lv = _dot(qe[rows, hs], ke[rows, hs], NT_DIMS)
            sc_ref[c, h] = jnp.where(lvl == lv, s_lv, sc_ref[c, h])
    for c, h, rows, hs in tiles:
        acc_ref[rows, hs] = _dot(sc_ref[c, h].astype(BF16), vb[rows, hs])
    for c, h, rows, hs in tiles:
        st = st_ref[h]
        kv_c = kv_ref[c, h]
        kv_ref[c, h] = st
        st_ref[h] = st * cum_e[(c + 1) * ch - 1:(c + 1) * ch, hs] + kv_c
    for c, h, rows, hs in tiles:
        acc_ref[rows, hs] += _dot(qc[rows, hs], kv_ref[c, h].astype(BF16), NT_DIMS)
    gate = gain_ref[...] * _silu(g_ref[0])
    for h in range(HG_HEADS):
        hs = slice(h * HG_DIM, (h + 1) * HG_DIM)
        o_ref[0, :, hs] = (_rms_scale(acc_ref[:, hs]) * gate[:, hs]).astype(o_ref.dtype)


def _hgrn2(hg, lb_logits, gain, layer, tb=512):
    b, s, _ = hg.shape
    w = HG_HEADS * HG_DIM
    stack, level = _hgrn_tables()
    blk = lambda k: pl.BlockSpec((1, tb, w), lambda bi, i: (bi, i, k))
    nch = tb // HG_CHUNK
    return pl.pallas_call(
        functools.partial(_hgrn_kernel, layer=layer, nch=nch),
        grid=(b, s // tb),
        in_specs=[blk(0), blk(1), blk(2), blk(3),
                  pl.BlockSpec(lb_logits.shape, lambda bi, i: (0, 0)),
                  pl.BlockSpec((1, w), lambda bi, i: (0, 0)),
                  pl.BlockSpec(stack.shape, lambda bi, i: (0, 0)),
                  pl.BlockSpec(level.shape, lambda bi, i: (0, 0))],
        out_specs=pl.BlockSpec((1, tb, w), lambda bi, i: (bi, i, 0)),
        out_shape=jax.ShapeDtypeStruct((b, s, w), BF16),
        scratch_shapes=[pltpu.VMEM((HG_HEADS, HG_DIM, HG_DIM), F32),
                        pltpu.VMEM((2 + HG_LEVELS, tb, w), F32),
                        pltpu.VMEM((nch, HG_HEADS, HG_CHUNK, HG_CHUNK), F32),
                        pltpu.VMEM((nch, HG_HEADS, HG_DIM, HG_DIM), F32),
                        pltpu.VMEM((tb, w), F32)],
        compiler_params=_params(("arbitrary", "arbitrary"), 40),
        name="hgrn2",
    )(hg, hg, hg, hg, lb_logits, gain, jnp.asarray(stack, BF16), jnp.asarray(level))


def _ffn_chunks(dff, width=512):
    edges = list(range(0, dff, width)) + [dff]
    return tuple(zip(edges[:-1], edges[1:]))


def _ffn_residual(x1, mod_ref, gain_ref, wg_ref, wu_ref, wo_ref, a_ref):
    h = _norm_mod(x1, gain_ref[...], mod_ref[0, 4:5, :], mod_ref[0, 3:4, :]).astype(BF16)
    for c0, c1 in _ffn_chunks(wg_ref.shape[1]):
        g = _dot(h, wg_ref[:, c0:c1])
        u = _dot(h, wu_ref[:, c0:c1])
        a_ref[:, c0:c1] = (_silu(g) * u).astype(BF16)
    return x1 + mod_ref[0, 5:6, :] * _dot(a_ref[...], wo_ref[...])


def _hybrid_out_ffn_kernel(x_ref, oa_ref, ob_ref, mod_ref, gain_ref, wa_ref, wb_ref, wg_ref, wu_ref, wo_ref,
                           o_ref, a_ref):
    mix = _dot(oa_ref[0], wa_ref[...]) + _dot(ob_ref[0], wb_ref[...])
    x1 = x_ref[0] + mod_ref[0, 2:3, :] * mix
    o_ref[0] = _ffn_residual(x1, mod_ref, gain_ref, wg_ref, wu_ref, wo_ref, a_ref)


def _hybrid_out_ffn(x, o_a, o_b, mod, gain_f, w_a, w_b, wg, wu, wo, tm=512):
    b, s, d = x.shape
    dff = wg.shape[1]
    row = lambda n: pl.BlockSpec((1, tm, n), lambda bi, i: (bi, i, 0))
    const = lambda shape: _resident(shape, lambda bi, i: (0, 0))
    return pl.pallas_call(
        _hybrid_out_ffn_kernel,
        grid=(b, s // tm),
        in_specs=[row(d), row(o_a.shape[2]), row(o_b.shape[2]),
                  pl.BlockSpec((1, N_MOD, d), lambda bi, i: (bi, 0, 0)),
                  pl.BlockSpec((1, d), lambda bi, i: (0, 0)),
                  const(w_a.shape), const(w_b.shape), const(wg.shape), const(wu.shape), const(wo.shape)],
        out_specs=row(d),
        out_shape=jax.ShapeDtypeStruct((b, s, d), F32),
        scratch_shapes=[pltpu.VMEM((tm, dff), BF16)],
        compiler_params=_params(("arbitrary", "arbitrary"), 56),
        name="hybrid_out_ffn",
    )(x, o_a, o_b, mod, gain_f, w_a, w_b, wg, wu, wo)


def _s5_glu_ffn_kernel(x_ref, y_ref, mod_ref, gain_ref, fin_ref, wv_ref, wt_ref, wg_ref, wu_ref, wo_ref,
                       o_ref, a_ref):
    y = y_ref[0]
    mix = _dot(y, wv_ref[...]) * _sigmoid(_dot(y, wt_ref[...]))
    x1 = x_ref[0] + mod_ref[0, 2:3, :] * mix
    x2 = _ffn_residual(x1, mod_ref, gain_ref, wg_ref, wu_ref, wo_ref, a_ref)
    o_ref[0] = _rms_scale(x2) * fin_ref[...]


def _s5_glu_ffn(x, y, mod, gain_f, final_g, wv, wt, wg, wu, wo, tm=512):
    b, s, d = x.shape
    dff = wg.shape[1]
    const = lambda shape: _resident(shape, lambda bi, i: (0, 0))
    return pl.pallas_call(
        _s5_glu_ffn_kernel,
        grid=(b, s // tm),
        in_specs=[pl.BlockSpec((1, tm, d), lambda bi, i: (bi, i, 0)),
                  pl.BlockSpec((1, tm, d), lambda bi, i: (bi, i, 0)),
                  pl.BlockSpec((1, N_MOD, d), lambda bi, i: (bi, 0, 0)),
                  pl.BlockSpec((1, d), lambda bi, i: (0, 0)),
                  pl.BlockSpec((1, d), lambda bi, i: (0, 0)),
                  const(wv.shape), const(wt.shape), const(wg.shape), const(wu.shape), const(wo.shape)],
        out_specs=pl.BlockSpec((1, tm, d), lambda bi, i: (bi, i, 0)),
        out_shape=jax.ShapeDtypeStruct((b, s, d), F32),
        scratch_shapes=[pltpu.VMEM((tm, dff), BF16)],
        compiler_params=_params(("arbitrary", "arbitrary"), 56),
        name="s5_glu_ffn",
    )(x, y, mod, gain_f, final_g, wv, wt, wg, wu, wo)


def _s5_discretise_kernel(lr_ref, li_ref, ldt_ref, br_ref, bi_ref, are_ref, aim_ref, bbr_ref, bbi_ref):
    lr = jnp.minimum(lr_ref[...], -1e-4)
    li = li_ref[...]
    dt = jnp.exp(ldt_ref[...])
    mag = jnp.exp(lr * dt)
    a_re = mag * jnp.cos(li * dt)
    a_im = mag * jnp.sin(li * dt)
    e_re = a_re - 1.0
    den = lr * lr + li * li
    z_re = (e_re * lr + a_im * li) / den
    z_im = (a_im * lr - e_re * li) / den
    are_ref[...] = a_re
    aim_ref[...] = a_im
    br = br_ref[...]
    bi = bi_ref[...]
    bbr_ref[...] = z_re[:, None, :] * br - z_im[:, None, :] * bi
    bbi_ref[...] = z_re[:, None, :] * bi + z_im[:, None, :] * br


def _s5_discretise(lam_re, lam_im, log_dt, b_re_t, b_im_t):
    g, p = lam_re.shape
    return pl.pallas_call(
        _s5_discretise_kernel,
        out_shape=[jax.ShapeDtypeStruct((g, p), F32), jax.ShapeDtypeStruct((g, p), F32),
                   jax.ShapeDtypeStruct(b_re_t.shape, F32), jax.ShapeDtypeStruct(b_re_t.shape, F32)],
        name="s5_discretise",
    )(lam_re, lam_im, log_dt.reshape(g, 1), b_re_t, b_im_t)


def _gelu_tanh(x):
    return 0.5 * x * (1.0 + jnp.tanh(math.sqrt(2.0 / math.pi) * (x + 0.044715 * (x * x * x))))


def _s5_scan_kernel(u_ref, bm_ref, cr_ref, ci_ref, are_ref, aim_ref, d_ref, o_ref,
                    u_sc, ub_sc, bu_ref, xr_ref, xi_ref, y_ref, st_ref, *, ts, nb, half, sub):
    @pl.when(pl.program_id(1) == 0)
    def _():
        st_ref[...] = jnp.zeros_like(st_ref)

    tsub = ts // sub
    rows = tsub * nb
    n_col = 2 * half // MXU_DIM
    n_k = half // MXU_DIM
    u_sc[...] = jnp.swapaxes(u_ref[...], 0, 1).reshape(ts * nb, LANES)
    ub_sc[...] = u_sc[...].astype(BF16)
    a_re = jnp.broadcast_to(are_ref[0], (nb, half))
    a_im = jnp.broadcast_to(aim_ref[0], (nb, half))

    def bu_tile(j, n):
        cols = slice(n * MXU_DIM, (n + 1) * MXU_DIM)
        bu_ref[j % 2, :, cols] = _dot(ub_sc[j * rows:(j + 1) * rows, :], bm_ref[0, :, cols])

    def y_tile(j, n):
        ks = slice((n % n_k) * MXU_DIM, (n % n_k + 1) * MXU_DIM)
        if n < n_k:
            part = _dot(xr_ref[j % 2, :, ks].astype(BF16), cr_ref[0, ks, :])
            y_ref[...] = part if n == 0 else y_ref[...] + part
        else:
            y_ref[...] -= _dot(xi_ref[j % 2, :, ks].astype(BF16), ci_ref[0, ks, :])

    def finish(j):
        y = _gelu_tanh(y_ref[...] + d_ref[0] * u_sc[j * rows:(j + 1) * rows, :])
        o_ref[:, j * tsub:(j + 1) * tsub, :] = jnp.swapaxes(y.reshape(tsub, nb, LANES), 0, 1).astype(o_ref.dtype)

    def scan_piece(j, n, state):
        x_re, x_im = state
        for t in range(n * tsub // n_col, (n + 1) * tsub // n_col):
            r = slice(t * nb, (t + 1) * nb)
            x_re, x_im = (a_re * x_re - a_im * x_im + bu_ref[j % 2, r, 0:half],
                          a_re * x_im + a_im * x_re + bu_ref[j % 2, r, half:2 * half])
            xr_ref[j % 2, r, :] = x_re
            xi_ref[j % 2, r, :] = x_im
        return x_re, x_im

    for n in range(n_col):
        bu_tile(0, n)
    state = (st_ref[:, 0:half], st_ref[:, half:2 * half])
    for j in range(sub):
        for n in range(n_col):
            state = scan_piece(j, n, state)
            if j + 1 < sub:
                bu_tile(j + 1, n)
            if j >= 1:
                y_tile(j - 1, n)
        if j >= 1:
            finish(j - 1)
    for n in range(n_col):
        y_tile(sub - 1, n)
    finish(sub - 1)
    st_ref[:, 0:half] = state[0]
    st_ref[:, half:2 * half] = state[1]


def _s5_scan(u, bmat, cr, ci, a_re, a_im, d_skip, ts=512, sub=4):
    nb, s, d = u.shape
    slabs = d // LANES
    half = bmat.shape[2] // 2
    rows = ts // sub * nb
    io = pl.BlockSpec((nb, ts, LANES), lambda o, i: (0, i, o))
    per_slab = lambda shape: pl.BlockSpec((1,) + shape, lambda o, i: (o, 0, 0))
    return pl.pallas_call(
        functools.partial(_s5_scan_kernel, ts=ts, nb=nb, half=half, sub=sub),
        grid=(slabs, s // ts),
        in_specs=[io, per_slab(bmat.shape[1:]), per_slab(cr.shape[1:]), per_slab(ci.shape[1:]),
                  per_slab((1, half)), per_slab((1, half)), per_slab((1, LANES))],
        out_specs=io,
        out_shape=jax.ShapeDtypeStruct((nb, s, d), BF16),
        scratch_shapes=[pltpu.VMEM((ts * nb, LANES), F32), pltpu.VMEM((ts * nb, LANES), BF16),
                        pltpu.VMEM((2, rows, 2 * half), F32),
                        pltpu.VMEM((2, rows, half), F32), pltpu.VMEM((2, rows, half), F32),
                        pltpu.VMEM((rows, LANES), F32), pltpu.VMEM((nb, 2 * half), F32)],
        compiler_params=_params(("arbitrary", "arbitrary"), 48),
        name="s5_scan",
    )(u, bmat, cr, ci, a_re, a_im, d_skip)


def _block_diag_slabs(t, per_slab):
    g, a, b = t.shape
    t = t.reshape(g // per_slab, per_slab, a, b)
    eye = jnp.eye(per_slab, dtype=t.dtype)
    return jnp.einsum('ogab,gk->ogakb', t, eye).reshape(g // per_slab, per_slab * a, per_slab * b)


def kernel(x, c, norm_mix_g, norm_ffn_g, ada_w, ada_b, ffn_w_in, ffn_w_out, final_norm_g, hy_w_in, hy_w_out, hg_norm_g, hg_lb_logits, s5_w_in, s5_lam_re, s5_lam_im, s5_log_dt, s5_b_re, s5_b_im, s5_c_re, s5_c_im, s5_d, s5_w_glu):
    b, s, d = x.shape
    depth = ada_w.shape[0]
    dff = ffn_w_out.shape[1]
    sb_w = SB_HEADS * SB_HEAD_DIM
    mod = _adaln_mod(c, ada_w, ada_b).reshape(depth, b, N_MOD, d)
    per_slab = LANES // S5_GROUP

    for layer in range(depth):
        i = layer // 2
        wg = ffn_w_in[layer, :, :dff].astype(BF16)
        wu = ffn_w_in[layer, :, dff:].astype(BF16)
        wo = ffn_w_out[layer].astype(BF16)
        gain_m = norm_mix_g[layer].reshape(1, d)
        gain_f = norm_ffn_g[layer].reshape(1, d)
        if layer % 2 == 0:
            qkv, hg = _hybrid_in_proj(x, mod[layer], gain_m, hy_w_in[i].astype(BF16))
            o_a = _sb_attention(qkv)
            o_b = _hgrn2(hg, hg_lb_logits, hg_norm_g[i].reshape(1, -1), layer)
            w_out = hy_w_out[i].astype(BF16)
            x = _hybrid_out_ffn(x, o_a, o_b, mod[layer], gain_f, w_out[:sb_w], w_out[sb_w:], wg, wu, wo)
        else:
            u = _s5_in_proj(x, mod[layer], gain_m, s5_w_in[i].astype(BF16))
            a_re, a_im, bb_re, bb_im = _s5_discretise(
                s5_lam_re[i], s5_lam_im[i], s5_log_dt[i],
                jnp.swapaxes(s5_b_re[i], 1, 2), jnp.swapaxes(s5_b_im[i], 1, 2))
            slabs = d // LANES
            bmat = jnp.concatenate([_block_diag_slabs(bb_re, per_slab), _block_diag_slabs(bb_im, per_slab)],
                                   axis=2).astype(BF16)
            cr = _block_diag_slabs(jnp.swapaxes(s5_c_re[i], 1, 2), per_slab).astype(BF16)
            ci = _block_diag_slabs(jnp.swapaxes(s5_c_im[i], 1, 2), per_slab).astype(BF16)
            half = per_slab * S5_STATE
            y = _s5_scan(u, bmat, cr, ci, a_re.reshape(slabs, 1, half), a_im.reshape(slabs, 1, half),
                         s5_d[i].reshape(slabs, 1, LANES))
            w_glu = s5_w_glu[i].astype(BF16)
            last = layer == depth - 1
            x = _s5_glu_ffn(x, y, mod[layer], gain_f, final_norm_g.reshape(1, d),
                            w_glu[:, :d], w_glu[:, d:], wg, wu, wo)
            assert last, "final norm is fused into the S5 layer's FFN kernel"
    return x
```

```python
import functools
import math

import numpy as np
import jax
import jax.numpy as jnp
from jax import lax
from jax.experimental import pallas as pl
from jax.experimental.pallas import tpu as pltpu

F32 = jnp.float32
BF16 = jnp.bfloat16

EPS = 1e-6
N_MOD = 6
SB_HEADS = 8
SB_HEAD_DIM = 64
HG_HEADS = 4
HG_DIM = 128
HG_CHUNK = 64
S5_GROUP = 16
S5_STATE = 64

LANES = 128
SUBLANES = 8
MXU_DIM = 256
MIB = 1 << 20

SB_DEAD_LOG_WEIGHT = 88.0

NT_DIMS = (((1,), (1,)), ((), ()))
TN_DIMS = (((0,), (0,)), ((), ()))


def _dot(a, b, dims=None):
    if dims is None:
        return jnp.dot(a, b, preferred_element_type=F32)
    return lax.dot_general(a, b, dims, preferred_element_type=F32)


def _split_bf16(x, n):
    parts = []
    rest = x
    for i in range(n):
        p = rest.astype(BF16)
        parts.append(p)
        if i + 1 < n:
            rest = rest - p.astype(F32)
    return parts


def _sigmoid(x):
    return 1.0 / (1.0 + jnp.exp(-x))


def _silu(x):
    return x * _sigmoid(x)


def _rms_scale(x):
    return x * lax.rsqrt(jnp.mean(x * x, axis=-1, keepdims=True) + EPS)


def _norm_mod(x, gain, scale, shift):
    return _rms_scale(x) * (gain * (1.0 + scale)) + shift


def _params(semantics, vmem_mib):
    return pltpu.CompilerParams(dimension_semantics=semantics, vmem_limit_bytes=vmem_mib * MIB)


def _resident(shape, index_map):
    return pl.BlockSpec(shape, index_map, pipeline_mode=pl.Buffered(1))


def _mod_kernel(c_ref, w_ref, b_ref, o_ref):
    a_hi, a_lo = _split_bf16(_silu(c_ref[...]), 2)
    w_hi, w_lo = _split_bf16(w_ref[0], 2)
    acc = _dot(a_hi, w_hi) + _dot(a_hi, w_lo) + _dot(a_lo, w_hi)
    o_ref[0] = acc + b_ref[0]


def _adaln_mod(c, ada_w, ada_b, tn=1536):
    depth, d, n = ada_w.shape
    b = c.shape[0]
    return pl.pallas_call(
        _mod_kernel,
        grid=(depth, n // tn),
        in_specs=[pl.BlockSpec((b, d), lambda l, j: (0, 0)),
                  pl.BlockSpec((1, d, tn), lambda l, j: (l, 0, j)),
                  pl.BlockSpec((1, 1, tn), lambda l, j: (l, 0, j))],
        out_specs=pl.BlockSpec((1, b, tn), lambda l, j: (l, 0, j)),
        out_shape=jax.ShapeDtypeStruct((depth, b, n), F32),
        compiler_params=_params(("arbitrary", "arbitrary"), 40),
        name="adaln_mod",
    )(c, ada_w, ada_b.reshape(depth, 1, n))


def _normmod_proj_kernel(x_ref, mod_ref, gain_ref, w_ref, *o_refs, splits):
    x = x_ref[0]
    h = _norm_mod(x, gain_ref[...], mod_ref[0, 1:2, :], mod_ref[0, 0:1, :]).astype(BF16)
    for o_ref, (c0, c1) in zip(o_refs, splits):
        y = _dot(h, w_ref[:, c0:c1])
        o_ref[...] = y.reshape(o_ref.shape).astype(o_ref.dtype)


def _hybrid_in_proj(x, mod, gain, w, tm=512):
    b, s, d = x.shape
    n = w.shape[1]
    n_a = 3 * SB_HEADS * SB_HEAD_DIM
    return pl.pallas_call(
        functools.partial(_normmod_proj_kernel, splits=((0, n_a), (n_a, n))),
        grid=(b, s // tm),
        in_specs=[pl.BlockSpec((1, tm, d), lambda bi, i: (bi, i, 0)),
                  pl.BlockSpec((1, N_MOD, d), lambda bi, i: (bi, 0, 0)),
                  pl.BlockSpec((1, d), lambda bi, i: (0, 0)),
                  _resident((d, n), lambda bi, i: (0, 0))],
        out_specs=[pl.BlockSpec((1, tm, n_a), lambda bi, i: (bi, i, 0)),
                   pl.BlockSpec((1, tm, n - n_a), lambda bi, i: (bi, i, 0))],
        out_shape=[jax.ShapeDtypeStruct((b, s, n_a), BF16),
                   jax.ShapeDtypeStruct((b, s, n - n_a), F32)],
        compiler_params=_params(("arbitrary", "arbitrary"), 48),
        name="hybrid_in_proj",
    )(x, mod, gain, w)


SB_BAND = 2 * LANES
SB_TAIL_ROWS = 2 * LANES


def _sb_log_terms(qq, kj, mask, u2, row_sum_on_lanes=False):
    z = _dot(qq, kj, NT_DIMS)
    log_beta = jnp.minimum(z, 0.0) - jnp.log(1.0 + jnp.exp(-jnp.abs(z)))
    log_keep = jnp.where(mask, log_beta - z, 0.0)
    hi, lo = _split_bf16(log_keep, 2)
    later = _dot(hi, u2) + _dot(lo, u2)
    row_sum = jnp.sum(log_keep, axis=1, keepdims=True)
    if not row_sum_on_lanes:
        return log_beta, later, row_sum
    ones = jnp.ones((SUBLANES, SB_BAND), BF16)
    return log_beta, later, row_sum, _dot(ones, hi, NT_DIMS) + _dot(ones, lo, NT_DIMS)


def _sb_weights_dot_v(log_beta, later, carry, mask, vj):
    tail = later if carry is None else later + carry
    w = jnp.where(mask, jnp.exp(log_beta + tail), 0.0)
    return _dot(w.astype(BF16), vj)


def _sb_band(qq, kj, vj, mask, carry, u2):
    log_beta, later, row_sum = _sb_log_terms(qq, kj, mask, u2)
    pv = _sb_weights_dot_v(log_beta, later, carry, mask, vj)
    return pv, row_sum if carry is None else carry + row_sum


class _SbBlock:
    pass


def _sb_attn_kernel(q_ref, k_ref, v_ref, u2_ref, ur_ref, o_ref, acc_ref, carry_ref, tacc_ref, tcarry_ref,
                    *, tq, dh, scale, pairs, nblk):
    rows = 2 * tq
    cap = SB_TAIL_ROWS
    u2 = u2_ref[...]
    lane = lax.broadcasted_iota(jnp.int32, (tq, LANES), 1)
    row = lax.broadcasted_iota(jnp.int32, (rows, SB_BAND), 0)
    col = lax.broadcasted_iota(jnp.int32, (rows, SB_BAND), 1)
    rel = jnp.where(row >= tq, row - tq, row) - col
    col_c = lax.broadcasted_iota(jnp.int32, (cap, SB_BAND), 1)
    slot = lax.broadcasted_iota(jnp.int32, (cap, rows), 0).astype(F32)
    slot_l = lax.broadcasted_iota(jnp.int32, (cap, LANES), 0).astype(F32)
    ones = jnp.ones((rows, LANES), BF16)
    lanes = [slice(p * LANES, (p + 1) * LANES) for p in range(pairs)]

    def live(state):
        off_prev, cmax = state
        return jnp.logical_and(off_prev > 0, cmax > -SB_DEAD_LOG_WEIGHT)

    def earlier_offset(off_prev):
        return pl.multiple_of(jnp.maximum(off_prev - SB_BAND, 0), LANES)

    def start(blk):
        s = _SbBlock()
        s.blk = blk
        s.q0 = (pl.program_id(1) * nblk + blk) * tq
        s.qqs = []
        for p in range(pairs):
            qs = q_ref[0, blk * tq:(blk + 1) * tq, lanes[p]] * jnp.asarray(scale, q_ref.dtype)
            zero = jnp.zeros_like(qs)
            s.qqs.append(jnp.concatenate([jnp.where(lane < dh, qs, zero), jnp.where(lane >= dh, qs, zero)], axis=0))
        s.off0 = pl.multiple_of(jnp.maximum(s.q0 + tq - SB_BAND, 0), LANES)
        s.causal = rel > (s.off0 - s.q0)
        s.carry_ts, s.carry_max = [], None
        return s

    def first_band(s, p):
        kj = k_ref[0, pl.ds(s.off0, SB_BAND), lanes[p]]
        vj = v_ref[0, pl.ds(s.off0, SB_BAND), lanes[p]]
        log_beta, later, carry, carry_t = _sb_log_terms(s.qqs[p], kj, s.causal, u2, row_sum_on_lanes=True)
        acc_ref[s.blk, p] = _sb_weights_dot_v(log_beta, later, None, s.causal, vj)
        carry_ref[s.blk, p] = carry
        s.carry_ts.append(carry_t)
        s.carry_max = carry if s.carry_max is None else jnp.maximum(s.carry_max, carry)

    def select(s):
        alives = [t > -SB_DEAD_LOG_WEIGHT for t in s.carry_ts]
        alive_bs = [jnp.where(a, 1.0, 0.0).astype(BF16) for a in alives]
        ranks = [_dot(a, ur_ref[...]) for a in alive_bs]
        counts = [_dot(a, ones) for a in alive_bs]
        total = jnp.zeros((SUBLANES, LANES), F32)
        c_c, s.sels = None, []
        for p in range(pairs):
            rank = ranks[p] + jnp.concatenate([total] * (rows // LANES), axis=1)
            hit = jnp.logical_and(rank[0:1, :] == slot, alives[p][0:1, :])
            s.sels.append(jnp.where(hit, 1.0, 0.0).astype(BF16))
            picked = jnp.sum(jnp.where(hit, s.carry_ts[p][0:1, :], 0.0), axis=1, keepdims=True)
            c_c = picked if c_c is None else c_c + picked
            total = total + counts[p]
        s.q_c = jnp.concatenate([_dot(s.sels[p], s.qqs[p]) for p in range(pairs)], axis=1).astype(BF16)
        s.tcarry0 = jnp.where(slot_l[:, 0:1] < total[0:1, 0:1], c_c, -1e30)
        s.fits = jnp.broadcast_to(total[0:1, :], (cap, LANES)) <= cap
        s.cmax0 = jnp.max(s.carry_max)
        s.n_live = jnp.max(total)

    def two_bands_logs(s, off_prev):
        off1 = earlier_offset(off_prev)
        off2 = earlier_offset(off1)
        fresh1 = col_c < (off_prev - off1)
        fresh2 = col_c < (off1 - off2)
        t1 = _sb_log_terms(s.q_c, k_ref[0, pl.ds(off1, SB_BAND), :], fresh1, u2)
        t2 = _sb_log_terms(s.q_c, k_ref[0, pl.ds(off2, SB_BAND), :], fresh2, u2)
        return off1, off2, fresh1, fresh2, t1, t2

    def two_bands_weights(logs, carry0):
        off1, off2, fresh1, fresh2, (lb1, later1, sum1), (lb2, later2, sum2) = logs
        carry1 = carry0 + sum1
        pv1 = _sb_weights_dot_v(lb1, later1, carry0, fresh1, v_ref[0, pl.ds(off1, SB_BAND), :])
        pv2 = _sb_weights_dot_v(lb2, later2, carry1, fresh2, v_ref[0, pl.ds(off2, SB_BAND), :])
        return pv1 + pv2, carry1 + sum2, off2

    def scatter(s, pv):
        for p in range(pairs):
            add = None
            for part in _split_bf16(jnp.where(s.fits, pv[:, lanes[p]], 0.0), 2):
                d = _dot(s.sels[p], part, TN_DIMS)
                add = d if add is None else add + d
            acc_ref[s.blk, p] += add

    def compact_walk(s):
        def logs():
            s.logs = two_bands_logs(s, s.off0)

        def weights():
            s.pv, s.tcarry1, s.off1 = two_bands_weights(s.logs, s.tcarry0)

        def back():
            scatter(s, s.pv)
            s.cmax1 = jnp.max(s.tcarry1)

        return [logs, weights, back]

    blocks, pending = [], []
    for blk in range(nblk):
        s = start(blk)
        for p in range(pairs):
            first_band(s, p)
            if p < len(pending):
                pending[p]()
        select(s)
        blocks.append(s)
        pending = compact_walk(s)
    for part in pending:
        part()

    def walk_live_rows(s):
        tcarry_ref[...] = s.tcarry1
        tacc_ref[...] = jnp.zeros_like(tacc_ref)

        def two_earlier_bands(state):
            off_prev, _ = state
            pv, carry, off = two_bands_weights(two_bands_logs(s, off_prev), tcarry_ref[...])
            tacc_ref[...] += pv
            tcarry_ref[...] = carry
            return off, jnp.max(carry)

        lax.while_loop(live, two_earlier_bands, (s.off1, s.cmax1))
        scatter(s, tacc_ref[...])

    def walk_all_rows(s):
        def earlier_band(state):
            off_prev, _ = state
            off = earlier_offset(off_prev)
            fresh = col < (off_prev - off)
            cmax = None
            for p in range(pairs):
                kj = k_ref[0, pl.ds(off, SB_BAND), lanes[p]]
                vj = v_ref[0, pl.ds(off, SB_BAND), lanes[p]]
                pv, carry = _sb_band(s.qqs[p], kj, vj, fresh, carry_ref[s.blk, p], u2)
                acc_ref[s.blk, p] += pv
                carry_ref[s.blk, p] = carry
                cmax = carry if cmax is None else jnp.maximum(cmax, carry)
            return off, jnp.max(cmax)

        lax.while_loop(live, earlier_band, (s.off0, s.cmax0))

    for s in blocks:
        fits = s.n_live <= cap

        @pl.when(jnp.logical_and(fits, live((s.off1, s.cmax1))))
        def _():
            walk_live_rows(s)

        @pl.when(jnp.logical_and(jnp.logical_not(fits), live((s.off0, s.cmax0))))
        def _():
            walk_all_rows(s)

    for s in blocks:
        for p in range(pairs):
            acc = acc_ref[s.blk, p]
            o_ref[0, s.blk * tq:(s.blk + 1) * tq, lanes[p]] = (
                jnp.where(lane < dh, acc[:tq], acc[tq:]).astype(o_ref.dtype))


def _sb_attention(qkv, tq=LANES, nblk=2):
    b, s, _ = qkv.shape
    width = SB_HEADS * SB_HEAD_DIM
    pairs = width // LANES
    j = np.arange(SB_BAND)
    u2 = (j[:, None] > j[None, :]).astype(np.float32)
    r = np.arange(2 * tq)
    ur = (r[:, None] < r[None, :]).astype(np.float32)
    return pl.pallas_call(
        functools.partial(_sb_attn_kernel, tq=tq, dh=SB_HEAD_DIM, scale=SB_HEAD_DIM ** -0.5, pairs=pairs,
                          nblk=nblk),
        grid=(b, s // (nblk * tq)),
        in_specs=[pl.BlockSpec((1, nblk * tq, width), lambda bi, i: (bi, i, 0)),
                  pl.BlockSpec((1, s, width), lambda bi, i: (bi, 0, 1)),
                  pl.BlockSpec((1, s, width), lambda bi, i: (bi, 0, 2)),
                  pl.BlockSpec(u2.shape, lambda bi, i: (0, 0)),
                  pl.BlockSpec(ur.shape, lambda bi, i: (0, 0))],
        out_specs=pl.BlockSpec((1, nblk * tq, width), lambda bi, i: (bi, i, 0)),
        out_shape=jax.ShapeDtypeStruct((b, s, width), BF16),
        scratch_shapes=[pltpu.VMEM((nblk, pairs, 2 * tq, LANES), F32), pltpu.VMEM((nblk, pairs, 2 * tq, 1), F32),
                        pltpu.VMEM((SB_TAIL_ROWS, width), F32), pltpu.VMEM((SB_TAIL_ROWS, 1), F32)],
        compiler_params=_params(("arbitrary", "arbitrary"), 48),
        name="sb_attention",
    )(qkv, qkv, qkv, jnp.asarray(u2, BF16), jnp.asarray(ur, BF16))


HG_LEVELS = int(math.log2(HG_CHUNK))
HG_DIAG = HG_LEVELS


def _hgrn_tables():
    c = HG_CHUNK
    t = np.arange(c)[:, None]
    j = np.arange(c)[None, :]
    mats = [(j <= t), (j > t)]
    for lv in range(HG_LEVELS):
        m = 1 << lv
        mid = (t // (2 * m)) * (2 * m) + m - 1
        later = (t % (2 * m)) >= m
        mats.append(np.where(later, (j > mid) & (j <= t), (j > t) & (j <= mid)))
    stack = np.concatenate(mats, axis=0).astype(np.float32)
    stack = np.concatenate([stack, stack], axis=1)
    s = j
    level = np.full((c, c), -1, np.int32)
    for lv in range(HG_LEVELS):
        level[(t > s) & ((t ^ s) >> lv == 1)] = lv
    level[t == s] = HG_DIAG
    return stack, level


def _hgrn_kernel(q_ref, f_ref, i_ref, g_ref, lbl_ref, gain_ref, gs_ref, lvl_ref, o_ref,
                 st_ref, ex_ref, sc_ref, kv_ref, acc_ref, *, layer, nch):
    ch = HG_CHUNK

    @pl.when(pl.program_id(1) == 0)
    def _():
        st_ref[...] = jnp.zeros_like(st_ref)

    logits = lbl_ref[...]
    ex = jnp.exp(logits - jnp.max(logits, axis=0, keepdims=True))
    lb = jnp.sum(ex[:layer + 1], axis=0, keepdims=True) / jnp.sum(ex, axis=0, keepdims=True)
    gs = gs_ref[...]
    lvl = lvl_ref[...]
    tiles = [(c, h, slice(c * ch, (c + 1) * ch), slice(h * HG_DIM, (h + 1) * HG_DIM))
             for c in range(nch) for h in range(HG_HEADS)]

    f = f_ref[0]
    e = jnp.exp(-jnp.abs(f))
    inv = 1.0 / (1.0 + e)
    sig_f = jnp.where(f >= 0, inv, e * inv)
    sig_nf = jnp.where(f >= 0, e * inv, inv)
    log_hi, log_lo = _split_bf16(jnp.log(lb + (1.0 - lb) * sig_f), 2)
    key = (1.0 - lb) * sig_nf
    for c in range(nch):
        rows = slice(c * ch, (c + 1) * ch)
        ex_c = jnp.exp(_dot(gs, jnp.concatenate([log_hi[rows], log_lo[rows]], axis=0)))
        for g in range(2 + HG_LEVELS):
            ex_ref[g, rows, :] = ex_c[g * ch:(g + 1) * ch]
    qa = _silu(q_ref[0])
    vb = i_ref[0].astype(BF16)
    cum_e = ex_ref[0]
    qc = (qa * cum_e).astype(BF16)
    kd = (key * ex_ref[1]).astype(BF16)
    qk = qa * key
    for c, h, rows, hs in tiles:
        sc_ref[c, h] = jnp.where(lvl == HG_DIAG, jnp.sum(qk[rows, hs], axis=1, keepdims=True), 0.0)
        kv_ref[c, h] = _dot(vb[rows, hs], kd[rows, hs], TN_DIMS)
    for lv in range(HG_LEVELS):
        p = ex_ref[2 + lv]
        qe = (qa * p).astype(BF16)
        ke = (key * p).astype(BF16)
        for c, h, rows, hs in tiles:
            s_lv = _dot(qe[rows, hs], ke[rows, hs], NT_DIMS)
            sc_ref[c, h] = jnp.where(lvl == lv, s_lv, sc_ref[c, h])
    for c, h, rows, hs in tiles:
        acc_ref[rows, hs] = _dot(sc_ref[c, h].astype(BF16), vb[rows, hs])
    for c, h, rows, hs in tiles:
        st = st_ref[h]
        kv_c = kv_ref[c, h]
        kv_ref[c, h] = st
        st_ref[h] = st * cum_e[(c + 1) * ch - 1:(c + 1) * ch, hs] + kv_c
    for c, h, rows, hs in tiles:
        acc_ref[rows, hs] += _dot(qc[rows, hs], kv_ref[c, h].astype(BF16), NT_DIMS)
    gate = gain_ref[...] * _silu(g_ref[0])
    for h in range(HG_HEADS):
        hs = slice(h * HG_DIM, (h + 1) * HG_DIM)
        o_ref[0, :, hs] = (_rms_scale(acc_ref[:, hs]) * gate[:, hs]).astype(o_ref.dtype)


def _hgrn2(hg, lb_logits, gain, layer, tb=512):
    b, s, _ = hg.shape
    w = HG_HEADS * HG_DIM
    stack, level = _hgrn_tables()
    blk = lambda k: pl.BlockSpec((1, tb, w), lambda bi, i: (bi, i, k))
    nch = tb // HG_CHUNK
    return pl.pallas_call(
        functools.partial(_hgrn_kernel, layer=layer, nch=nch),
        grid=(b, s // tb),
        in_specs=[blk(0), blk(1), blk(2), blk(3),
                  pl.BlockSpec(lb_logits.shape, lambda bi, i: (0, 0)),
                  pl.BlockSpec((1, w), lambda bi, i: (0, 0)),
                  pl.BlockSpec(stack.shape, lambda bi, i: (0, 0)),
                  pl.BlockSpec(level.shape, lambda bi, i: (0, 0))],
        out_specs=pl.BlockSpec((1, tb, w), lambda bi, i: (bi, i, 0)),
        out_shape=jax.ShapeDtypeStruct((b, s, w), BF16),
        scratch_shapes=[pltpu.VMEM((HG_HEADS, HG_DIM, HG_DIM), F32),
                        pltpu.VMEM((2 + HG_LEVELS, tb, w), F32),
                        pltpu.VMEM((nch, HG_HEADS, HG_CHUNK, HG_CHUNK), F32),
                        pltpu.VMEM((nch, HG_HEADS, HG_DIM, HG_DIM), F32),
                        pltpu.VMEM((tb, w), F32)],
        compiler_params=_params(("arbitrary", "arbitrary"), 40),
        name="hgrn2",
    )(hg, hg, hg, hg, lb_logits, gain, jnp.asarray(stack, BF16), jnp.asarray(level))


def _ffn_chunks(dff, width=512):
    edges = list(range(0, dff, width)) + [dff]
    return tuple(zip(edges[:-1], edges[1:]))


def _ffn_residual(x1, mod_ref, gain_ref, wg_ref, wu_ref, wo_ref, a_ref):
    h = _norm_mod(x1, gain_ref[...], mod_ref[0, 4:5, :], mod_ref[0, 3:4, :]).astype(BF16)
    for c0, c1 in _ffn_chunks(wg_ref.shape[1]):
        g = _dot(h, wg_ref[:, c0:c1])
        u = _dot(h, wu_ref[:, c0:c1])
        a_ref[:, c0:c1] = (_silu(g) * u).astype(BF16)
    return x1 + mod_ref[0, 5:6, :] * _dot(a_ref[...], wo_ref[...])


def _hybrid_out_ffn_kernel(x_ref, oa_ref, ob_ref, mod_ref, gain_ref, wa_ref, wb_ref, wg_ref, wu_ref, wo_ref,
                           nmod_ref, ngain_ref, nw_ref, o_ref, u_ref, a_ref):
    mix = _dot(oa_ref[0], wa_ref[...]) + _dot(ob_ref[0], wb_ref[...])
    x1 = x_ref[0] + mod_ref[0, 2:3, :] * mix
    x2 = _ffn_residual(x1, mod_ref, gain_ref, wg_ref, wu_ref, wo_ref, a_ref)
    o_ref[0] = x2
    h = _norm_mod(x2, ngain_ref[...], nmod_ref[0, 1:2, :], nmod_ref[0, 0:1, :]).astype(BF16)
    u_ref[0] = _dot(h, nw_ref[...])


def _hybrid_out_ffn(x, o_a, o_b, mod, gain_f, w_a, w_b, wg, wu, wo, next_mod, next_gain_m, next_w_in, tm=512):
    b, s, d = x.shape
    dff = wg.shape[1]
    row = lambda n: pl.BlockSpec((1, tm, n), lambda bi, i: (bi, i, 0))
    const = lambda shape: _resident(shape, lambda bi, i: (0, 0))
    per_batch = pl.BlockSpec((1, N_MOD, d), lambda bi, i: (bi, 0, 0))
    vec = pl.BlockSpec((1, d), lambda bi, i: (0, 0))
    return pl.pallas_call(
        _hybrid_out_ffn_kernel,
        grid=(b, s // tm),
        in_specs=[row(d), row(o_a.shape[2]), row(o_b.shape[2]), per_batch, vec,
                  const(w_a.shape), const(w_b.shape), const(wg.shape), const(wu.shape), const(wo.shape),
                  per_batch, vec, const(next_w_in.shape)],
        out_specs=[row(d), row(d)],
        out_shape=[jax.ShapeDtypeStruct((b, s, d), F32), jax.ShapeDtypeStruct((b, s, d), F32)],
        scratch_shapes=[pltpu.VMEM((tm, dff), BF16)],
        compiler_params=_params(("arbitrary", "arbitrary"), 56),
        name="hybrid_out_ffn",
    )(x, o_a, o_b, mod, gain_f, w_a, w_b, wg, wu, wo, next_mod, next_gain_m, next_w_in)


def _s5_glu_ffn_kernel(x_ref, y_ref, mod_ref, gain_ref, fin_ref, wv_ref, wt_ref, wg_ref, wu_ref, wo_ref,
                       o_ref, a_ref):
    y = y_ref[0]
    mix = _dot(y, wv_ref[...]) * _sigmoid(_dot(y, wt_ref[...]))
    x1 = x_ref[0] + mod_ref[0, 2:3, :] * mix
    x2 = _ffn_residual(x1, mod_ref, gain_ref, wg_ref, wu_ref, wo_ref, a_ref)
    o_ref[0] = _rms_scale(x2) * fin_ref[...]


def _s5_glu_ffn(x, y, mod, gain_f, final_g, wv, wt, wg, wu, wo, tm=512):
    b, s, d = x.shape
    dff = wg.shape[1]
    const = lambda shape: _resident(shape, lambda bi, i: (0, 0))
    return pl.pallas_call(
        _s5_glu_ffn_kernel,
        grid=(b, s // tm),
        in_specs=[pl.BlockSpec((1, tm, d), lambda bi, i: (bi, i, 0)),
                  pl.BlockSpec((1, tm, d), lambda bi, i: (bi, i, 0)),
                  pl.BlockSpec((1, N_MOD, d), lambda bi, i: (bi, 0, 0)),
                  pl.BlockSpec((1, d), lambda bi, i: (0, 0)),
                  pl.BlockSpec((1, d), lambda bi, i: (0, 0)),
                  const(wv.shape), const(wt.shape), const(wg.shape), const(wu.shape), const(wo.shape)],
        out_specs=pl.BlockSpec((1, tm, d), lambda bi, i: (bi, i, 0)),
        out_shape=jax.ShapeDtypeStruct((b, s, d), F32),
        scratch_shapes=[pltpu.VMEM((tm, dff), BF16)],
        compiler_params=_params(("arbitrary", "arbitrary"), 56),
        name="s5_glu_ffn",
    )(x, y, mod, gain_f, final_g, wv, wt, wg, wu, wo)


def _s5_discretise_kernel(lr_ref, li_ref, ldt_ref, br_ref, bi_ref, are_ref, aim_ref, bbr_ref, bbi_ref):
    lr = jnp.minimum(lr_ref[...], -1e-4)
    li = li_ref[...]
    dt = jnp.exp(ldt_ref[...])
    mag = jnp.exp(lr * dt)
    a_re = mag * jnp.cos(li * dt)
    a_im = mag * jnp.sin(li * dt)
    e_re = a_re - 1.0
    den = lr * lr + li * li
    z_re = (e_re * lr + a_im * li) / den
    z_im = (a_im * lr - e_re * li) / den
    are_ref[...] = a_re
    aim_ref[...] = a_im
    br = br_ref[...]
    bi = bi_ref[...]
    bbr_ref[...] = z_re[:, None, :] * br - z_im[:, None, :] * bi
    bbi_ref[...] = z_re[:, None, :] * bi + z_im[:, None, :] * br


def _s5_discretise(lam_re, lam_im, log_dt, b_re_t, b_im_t):
    g, p = lam_re.shape
    return pl.pallas_call(
        _s5_discretise_kernel,
        out_shape=[jax.ShapeDtypeStruct((g, p), F32), jax.ShapeDtypeStruct((g, p), F32),
                   jax.ShapeDtypeStruct(b_re_t.shape, F32), jax.ShapeDtypeStruct(b_re_t.shape, F32)],
        name="s5_discretise",
    )(lam_re, lam_im, log_dt.reshape(g, 1), b_re_t, b_im_t)


def _gelu_tanh(x):
    return 0.5 * x * (1.0 + jnp.tanh(math.sqrt(2.0 / math.pi) * (x + 0.044715 * (x * x * x))))


def _s5_scan_kernel(u_ref, bm_ref, cr_ref, ci_ref, are_ref, aim_ref, d_ref, o_ref,
                    u_sc, ub_sc, bu_ref, xr_ref, xi_ref, y_ref, st_ref, *, ts, nb, half, sub):
    @pl.when(pl.program_id(1) == 0)
    def _():
        st_ref[...] = jnp.zeros_like(st_ref)

    tsub = ts // sub
    rows = tsub * nb
    n_col = 2 * half // MXU_DIM
    n_k = half // MXU_DIM
    u_sc[...] = jnp.swapaxes(u_ref[...], 0, 1).reshape(ts * nb, LANES)
    ub_sc[...] = u_sc[...].astype(BF16)
    a_re = jnp.broadcast_to(are_ref[0], (nb, half))
    a_im = jnp.broadcast_to(aim_ref[0], (nb, half))

    def bu_tile(j, n):
        cols = slice(n * MXU_DIM, (n + 1) * MXU_DIM)
        bu_ref[j % 2, :, cols] = _dot(ub_sc[j * rows:(j + 1) * rows, :], bm_ref[0, :, cols])

    def y_tile(j, n):
        ks = slice((n % n_k) * MXU_DIM, (n % n_k + 1) * MXU_DIM)
        if n < n_k:
            part = _dot(xr_ref[j % 2, :, ks].astype(BF16), cr_ref[0, ks, :])
            y_ref[...] = part if n == 0 else y_ref[...] + part
        else:
            y_ref[...] -= _dot(xi_ref[j % 2, :, ks].astype(BF16), ci_ref[0, ks, :])

    def finish(j):
        y = _gelu_tanh(y_ref[...] + d_ref[0] * u_sc[j * rows:(j + 1) * rows, :])
        o_ref[:, j * tsub:(j + 1) * tsub, :] = jnp.swapaxes(y.reshape(tsub, nb, LANES), 0, 1).astype(o_ref.dtype)

    def scan_piece(j, n, state):
        x_re, x_im = state
        for t in range(n * tsub // n_col, (n + 1) * tsub // n_col):
            r = slice(t * nb, (t + 1) * nb)
            x_re, x_im = (a_re * x_re - a_im * x_im + bu_ref[j % 2, r, 0:half],
                          a_re * x_im + a_im * x_re + bu_ref[j % 2, r, half:2 * half])
            xr_ref[j % 2, r, :] = x_re
            xi_ref[j % 2, r, :] = x_im
        return x_re, x_im

    for n in range(n_col):
        bu_tile(0, n)
    state = (st_ref[:, 0:half], st_ref[:, half:2 * half])
    for j in range(sub):
        for n in range(n_col):
            state = scan_piece(j, n, state)
            if j + 1 < sub:
                bu_tile(j + 1, n)
            if j >= 1:
                y_tile(j - 1, n)
        if j >= 1:
            finish(j - 1)
    for n in range(n_col):
        y_tile(sub - 1, n)
    finish(sub - 1)
    st_ref[:, 0:half] = state[0]
    st_ref[:, half:2 * half] = state[1]


def _s5_scan(u, bmat, cr, ci, a_re, a_im, d_skip, ts=512, sub=4):
    nb, s, d = u.shape
    slabs = d // LANES
    half = bmat.shape[2] // 2
    rows = ts // sub * nb
    io = pl.BlockSpec((nb, ts, LANES), lambda o, i: (0, i, o))
    per_slab = lambda shape: pl.BlockSpec((1,) + shape, lambda o, i: (o, 0, 0))
    return pl.pallas_call(
        functools.partial(_s5_scan_kernel, ts=ts, nb=nb, half=half, sub=sub),
        grid=(slabs, s // ts),
        in_specs=[io, per_slab(bmat.shape[1:]), per_slab(cr.shape[1:]), per_slab(ci.shape[1:]),
                  per_slab((1, half)), per_slab((1, half)), per_slab((1, LANES))],
        out_specs=io,
        out_shape=jax.ShapeDtypeStruct((nb, s, d), BF16),
        scratch_shapes=[pltpu.VMEM((ts * nb, LANES), F32), pltpu.VMEM((ts * nb, LANES), BF16),
                        pltpu.VMEM((2, rows, 2 * half), F32),
                        pltpu.VMEM((2, rows, half), F32), pltpu.VMEM((2, rows, half), F32),
                        pltpu.VMEM((rows, LANES), F32), pltpu.VMEM((nb, 2 * half), F32)],
        compiler_params=_params(("arbitrary", "arbitrary"), 48),
        name="s5_scan",
    )(u, bmat, cr, ci, a_re, a_im, d_skip)


def _block_diag_slabs(t, per_slab):
    g, a, b = t.shape
    t = t.reshape(g // per_slab, per_slab, a, b)
    eye = jnp.eye(per_slab, dtype=t.dtype)
    return jnp.einsum('ogab,gk->ogakb', t, eye).reshape(g // per_slab, per_slab * a, per_slab * b)


def kernel(x, c, norm_mix_g, norm_ffn_g, ada_w, ada_b, ffn_w_in, ffn_w_out, final_norm_g, hy_w_in, hy_w_out, hg_norm_g, hg_lb_logits, s5_w_in, s5_lam_re, s5_lam_im, s5_log_dt, s5_b_re, s5_b_im, s5_c_re, s5_c_im, s5_d, s5_w_glu):
    b, s, d = x.shape
    depth = ada_w.shape[0]
    dff = ffn_w_out.shape[1]
    sb_w = SB_HEADS * SB_HEAD_DIM
    mod = _adaln_mod(c, ada_w, ada_b).reshape(depth, b, N_MOD, d)
    per_slab = LANES // S5_GROUP

    for layer in range(depth):
        i = layer // 2
        wg = ffn_w_in[layer, :, :dff].astype(BF16)
        wu = ffn_w_in[layer, :, dff:].astype(BF16)
        wo = ffn_w_out[layer].astype(BF16)
        gain_m = norm_mix_g[layer].reshape(1, d)
        gain_f = norm_ffn_g[layer].reshape(1, d)
        if layer % 2 == 0:
            assert layer + 1 < depth, "the hybrid layer's last kernel also projects the following S5 layer's input"
            qkv, hg = _hybrid_in_proj(x, mod[layer], gain_m, hy_w_in[i].astype(BF16))
            o_a = _sb_attention(qkv)
            o_b = _hgrn2(hg, hg_lb_logits, hg_norm_g[i].reshape(1, -1), layer)
            w_out = hy_w_out[i].astype(BF16)
            x, u = _hybrid_out_ffn(x, o_a, o_b, mod[layer], gain_f, w_out[:sb_w], w_out[sb_w:], wg, wu, wo,
                                   mod[layer + 1], norm_mix_g[layer + 1].reshape(1, d), s5_w_in[i].astype(BF16))
        else:
            a_re, a_im, bb_re, bb_im = _s5_discretise(
                s5_lam_re[i], s5_lam_im[i], s5_log_dt[i],
                jnp.swapaxes(s5_b_re[i], 1, 2), jnp.swapaxes(s5_b_im[i], 1, 2))
            slabs = d // LANES
            bmat = jnp.concatenate([_block_diag_slabs(bb_re, per_slab), _block_diag_slabs(bb_im, per_slab)],
                                   axis=2).astype(BF16)
            cr = _block_diag_slabs(jnp.swapaxes(s5_c_re[i], 1, 2), per_slab).astype(BF16)
            ci = _block_diag_slabs(jnp.swapaxes(s5_c_im[i], 1, 2), per_slab).astype(BF16)
            half = per_slab * S5_STATE
            y = _s5_scan(u, bmat, cr, ci, a_re.reshape(slabs, 1, half), a_im.reshape(slabs, 1, half),
                         s5_d[i].reshape(slabs, 1, LANES))
            w_glu = s5_w_glu[i].astype(BF16)
            last = layer == depth - 1
            x = _s5_glu_ffn(x, y, mod[layer], gain_f, final_norm_g.reshape(1, d),
                            w_glu[:, :d], w_glu[:, d:], wg, wu, wo)
            assert last, "final norm is fused into the S5 layer's FFN kernel"
    return x
```

```python
import functools
import math

import numpy as np
import jax
import jax.numpy as jnp
from jax import lax
from jax.experimental import pallas as pl
from jax.experimental.pallas import tpu as pltpu

F32 = jnp.float32
BF16 = jnp.bfloat16

EPS = 1e-6
N_MOD = 6
SB_HEADS = 8
SB_HEAD_DIM = 64
HG_HEADS = 4
HG_DIM = 128
HG_CHUNK = 64
S5_GROUP = 16
S5_STATE = 64

LANES = 128
SUBLANES = 8
MXU_DIM = 256
MIB = 1 << 20

SB_DEAD_LOG_WEIGHT = 88.0

NT_DIMS = (((1,), (1,)), ((), ()))
TN_DIMS = (((0,), (0,)), ((), ()))


def _dot(a, b, dims=None):
    if dims is None:
        return jnp.dot(a, b, preferred_element_type=F32)
    return lax.dot_general(a, b, dims, preferred_element_type=F32)


def _split_bf16(x, n):
    parts = []
    rest = x
    for i in range(n):
        p = rest.astype(BF16)
        parts.append(p)
        if i + 1 < n:
            rest = rest - p.astype(F32)
    return parts


def _sigmoid(x):
    return 1.0 / (1.0 + jnp.exp(-x))


def _silu(x):
    return x * _sigmoid(x)


def _rms_scale(x):
    return x * lax.rsqrt(jnp.mean(x * x, axis=-1, keepdims=True) + EPS)


def _norm_mod(x, gain, scale, shift):
    return _rms_scale(x) * (gain * (1.0 + scale)) + shift


def _params(semantics, vmem_mib):
    return pltpu.CompilerParams(dimension_semantics=semantics, vmem_limit_bytes=vmem_mib * MIB)


def _resident(shape, index_map):
    return pl.BlockSpec(shape, index_map, pipeline_mode=pl.Buffered(1))


def _mod_kernel(c_ref, w_ref, b_ref, o_ref):
    a_hi, a_lo = _split_bf16(_silu(c_ref[...]), 2)
    w_hi, w_lo = _split_bf16(w_ref[0], 2)
    acc = _dot(a_hi, w_hi) + _dot(a_hi, w_lo) + _dot(a_lo, w_hi)
    o_ref[0] = acc + b_ref[0]


def _adaln_mod(c, ada_w, ada_b, tn=1536):
    depth, d, n = ada_w.shape
    b = c.shape[0]
    return pl.pallas_call(
        _mod_kernel,
        grid=(depth, n // tn),
        in_specs=[pl.BlockSpec((b, d), lambda l, j: (0, 0)),
                  pl.BlockSpec((1, d, tn), lambda l, j: (l, 0, j)),
                  pl.BlockSpec((1, 1, tn), lambda l, j: (l, 0, j))],
        out_specs=pl.BlockSpec((1, b, tn), lambda l, j: (l, 0, j)),
        out_shape=jax.ShapeDtypeStruct((depth, b, n), F32),
        compiler_params=_params(("arbitrary", "arbitrary"), 40),
        name="adaln_mod",
    )(c, ada_w, ada_b.reshape(depth, 1, n))


def _normmod_proj_kernel(x_ref, mod_ref, gain_ref, w_ref, *o_refs, splits):
    x = x_ref[0]
    h = _norm_mod(x, gain_ref[...], mod_ref[0, 1:2, :], mod_ref[0, 0:1, :]).astype(BF16)
    for o_ref, (c0, c1) in zip(o_refs, splits):
        y = _dot(h, w_ref[:, c0:c1])
        o_ref[...] = y.reshape(o_ref.shape).astype(o_ref.dtype)


def _hybrid_in_proj(x, mod, gain, w, tm=512):
    b, s, d = x.shape
    n = w.shape[1]
    n_a = 3 * SB_HEADS * SB_HEAD_DIM
    return pl.pallas_call(
        functools.partial(_normmod_proj_kernel, splits=((0, n_a), (n_a, n))),
        grid=(b, s // tm),
        in_specs=[pl.BlockSpec((1, tm, d), lambda bi, i: (bi, i, 0)),
                  pl.BlockSpec((1, N_MOD, d), lambda bi, i: (bi, 0, 0)),
                  pl.BlockSpec((1, d), lambda bi, i: (0, 0)),
                  _resident((d, n), lambda bi, i: (0, 0))],
        out_specs=[pl.BlockSpec((1, tm, n_a), lambda bi, i: (bi, i, 0)),
                   pl.BlockSpec((1, tm, n - n_a), lambda bi, i: (bi, i, 0))],
        out_shape=[jax.ShapeDtypeStruct((b, s, n_a), BF16),
                   jax.ShapeDtypeStruct((b, s, n - n_a), F32)],
        compiler_params=_params(("arbitrary", "arbitrary"), 48),
        name="hybrid_in_proj",
    )(x, mod, gain, w)


SB_BAND = 2 * LANES
SB_TAIL_ROWS = 2 * LANES


def _sb_log_terms(qq, kj, mask, u2, row_sum_on_lanes=False):
    z = _dot(qq, kj, NT_DIMS)
    log_beta = jnp.minimum(z, 0.0) - jnp.log(1.0 + jnp.exp(-jnp.abs(z)))
    log_keep = jnp.where(mask, log_beta - z, 0.0)
    hi, lo = _split_bf16(log_keep, 2)
    later = _dot(hi, u2) + _dot(lo, u2)
    row_sum = jnp.sum(log_keep, axis=1, keepdims=True)
    if not row_sum_on_lanes:
        return log_beta, later, row_sum
    ones = jnp.ones((SUBLANES, SB_BAND), BF16)
    return log_beta, later, row_sum, _dot(ones, hi, NT_DIMS) + _dot(ones, lo, NT_DIMS)


def _sb_weights_dot_v(log_beta, later, carry, mask, vj):
    tail = later if carry is None else later + carry
    w = jnp.where(mask, jnp.exp(log_beta + tail), 0.0)
    return _dot(w.astype(BF16), vj)


def _sb_band(qq, kj, vj, mask, carry, u2):
    log_beta, later, row_sum = _sb_log_terms(qq, kj, mask, u2)
    pv = _sb_weights_dot_v(log_beta, later, carry, mask, vj)
    return pv, row_sum if carry is None else carry + row_sum


class _SbBlock:
    pass


def _sb_attn_kernel(q_ref, k_ref, v_ref, u2_ref, ur_ref, o_ref, acc_ref, carry_ref, tacc_ref, tcarry_ref,
                    *, tq, dh, scale, pairs, nblk):
    rows = 2 * tq
    cap = SB_TAIL_ROWS
    u2 = u2_ref[...]
    lane = lax.broadcasted_iota(jnp.int32, (tq, LANES), 1)
    row = lax.broadcasted_iota(jnp.int32, (rows, SB_BAND), 0)
    col = lax.broadcasted_iota(jnp.int32, (rows, SB_BAND), 1)
    rel = jnp.where(row >= tq, row - tq, row) - col
    col_c = lax.broadcasted_iota(jnp.int32, (cap, SB_BAND), 1)
    slot = lax.broadcasted_iota(jnp.int32, (cap, rows), 0).astype(F32)
    slot_l = lax.broadcasted_iota(jnp.int32, (cap, LANES), 0).astype(F32)
    ones = jnp.ones((rows, LANES), BF16)
    lanes = [slice(p * LANES, (p + 1) * LANES) for p in range(pairs)]

    def live(state):
        off_prev, cmax = state
        return jnp.logical_and(off_prev > 0, cmax > -SB_DEAD_LOG_WEIGHT)

    def earlier_offset(off_prev):
        return pl.multiple_of(jnp.maximum(off_prev - SB_BAND, 0), LANES)

    def start(blk):
        s = _SbBlock()
        s.blk = blk
        s.q0 = (pl.program_id(1) * nblk + blk) * tq
        s.qqs = []
        for p in range(pairs):
            qs = q_ref[0, blk * tq:(blk + 1) * tq, lanes[p]] * jnp.asarray(scale, q_ref.dtype)
            zero = jnp.zeros_like(qs)
            s.qqs.append(jnp.concatenate([jnp.where(lane < dh, qs, zero), jnp.where(lane >= dh, qs, zero)], axis=0))
        s.off0 = pl.multiple_of(jnp.maximum(s.q0 + tq - SB_BAND, 0), LANES)
        s.causal = rel > (s.off0 - s.q0)
        s.carry_ts, s.carry_max = [], None
        return s

    def first_band(s, p):
        kj = k_ref[0, pl.ds(s.off0, SB_BAND), lanes[p]]
        vj = v_ref[0, pl.ds(s.off0, SB_BAND), lanes[p]]
        log_beta, later, carry, carry_t = _sb_log_terms(s.qqs[p], kj, s.causal, u2, row_sum_on_lanes=True)
        acc_ref[s.blk, p] = _sb_weights_dot_v(log_beta, later, None, s.causal, vj)
        carry_ref[s.blk, p] = carry
        s.carry_ts.append(carry_t)
        s.carry_max = carry if s.carry_max is None else jnp.maximum(s.carry_max, carry)

    def select(s):
        alives = [t > -SB_DEAD_LOG_WEIGHT for t in s.carry_ts]
        alive_bs = [jnp.where(a, 1.0, 0.0).astype(BF16) for a in alives]
        ranks = [_dot(a, ur_ref[...]) for a in alive_bs]
        counts = [_dot(a, ones) for a in alive_bs]
        total = jnp.zeros((SUBLANES, LANES), F32)
        c_c, s.sels = None, []
        for p in range(pairs):
            rank = ranks[p] + jnp.concatenate([total] * (rows // LANES), axis=1)
            hit = jnp.logical_and(rank[0:1, :] == slot, alives[p][0:1, :])
            s.sels.append(jnp.where(hit, 1.0, 0.0).astype(BF16))
            picked = jnp.sum(jnp.where(hit, s.carry_ts[p][0:1, :], 0.0), axis=1, keepdims=True)
            c_c = picked if c_c is None else c_c + picked
            total = total + counts[p]
        s.q_c = jnp.concatenate([_dot(s.sels[p], s.qqs[p]) for p in range(pairs)], axis=1).astype(BF16)
        s.tcarry0 = jnp.where(slot_l[:, 0:1] < total[0:1, 0:1], c_c, -1e30)
        s.fits = jnp.broadcast_to(total[0:1, :], (cap, LANES)) <= cap
        s.cmax0 = jnp.max(s.carry_max)
        s.n_live = jnp.max(total)

    def two_bands_logs(s, off_prev):
        off1 = earlier_offset(off_prev)
        off2 = earlier_offset(off1)
        fresh1 = col_c < (off_prev - off1)
        fresh2 = col_c < (off1 - off2)
        t1 = _sb_log_terms(s.q_c, k_ref[0, pl.ds(off1, SB_BAND), :], fresh1, u2)
        t2 = _sb_log_terms(s.q_c, k_ref[0, pl.ds(off2, SB_BAND), :], fresh2, u2)
        return off1, off2, fresh1, fresh2, t1, t2

    def two_bands_weights(logs, carry0):
        off1, off2, fresh1, fresh2, (lb1, later1, sum1), (lb2, later2, sum2) = logs
        carry1 = carry0 + sum1
        pv1 = _sb_weights_dot_v(lb1, later1, carry0, fresh1, v_ref[0, pl.ds(off1, SB_BAND), :])
        pv2 = _sb_weights_dot_v(lb2, later2, carry1, fresh2, v_ref[0, pl.ds(off2, SB_BAND), :])
        return pv1 + pv2, carry1 + sum2, off2

    def scatter(s, pv):
        for p in range(pairs):
            add = None
            for part in _split_bf16(jnp.where(s.fits, pv[:, lanes[p]], 0.0), 2):
                d = _dot(s.sels[p], part, TN_DIMS)
                add = d if add is None else add + d
            acc_ref[s.blk, p] += add

    def compact_walk(s):
        def logs():
            s.logs = two_bands_logs(s, s.off0)

        def weights():
            s.pv, s.tcarry1, s.off1 = two_bands_weights(s.logs, s.tcarry0)

        def back():
            scatter(s, s.pv)
            s.cmax1 = jnp.max(s.tcarry1)

        return [logs, weights, back]

    blocks, pending = [], []
    for blk in range(nblk):
        s = start(blk)
        for p in range(pairs):
            first_band(s, p)
            if p < len(pending):
                pending[p]()
        select(s)
        blocks.append(s)
        pending = compact_walk(s)
    for part in pending:
        part()

    def walk_live_rows(s):
        tcarry_ref[...] = s.tcarry1
        tacc_ref[...] = jnp.zeros_like(tacc_ref)

        def two_earlier_bands(state):
            off_prev, _ = state
            pv, carry, off = two_bands_weights(two_bands_logs(s, off_prev), tcarry_ref[...])
            tacc_ref[...] += pv
            tcarry_ref[...] = carry
            return off, jnp.max(carry)

        lax.while_loop(live, two_earlier_bands, (s.off1, s.cmax1))
        scatter(s, tacc_ref[...])

    def walk_all_rows(s):
        def earlier_band(state):
            off_prev, _ = state
            off = earlier_offset(off_prev)
            fresh = col < (off_prev - off)
            cmax = None
            for p in range(pairs):
                kj = k_ref[0, pl.ds(off, SB_BAND), lanes[p]]
                vj = v_ref[0, pl.ds(off, SB_BAND), lanes[p]]
                pv, carry = _sb_band(s.qqs[p], kj, vj, fresh, carry_ref[s.blk, p], u2)
                acc_ref[s.blk, p] += pv
                carry_ref[s.blk, p] = carry
                cmax = carry if cmax is None else jnp.maximum(cmax, carry)
            return off, jnp.max(cmax)

        lax.while_loop(live, earlier_band, (s.off0, s.cmax0))

    for s in blocks:
        fits = s.n_live <= cap

        @pl.when(jnp.logical_and(fits, live((s.off1, s.cmax1))))
        def _():
            walk_live_rows(s)

        @pl.when(jnp.logical_and(jnp.logical_not(fits), live((s.off0, s.cmax0))))
        def _():
            walk_all_rows(s)

    for s in blocks:
        for p in range(pairs):
            acc = acc_ref[s.blk, p]
            o_ref[0, s.blk * tq:(s.blk + 1) * tq, lanes[p]] = (
                jnp.where(lane < dh, acc[:tq], acc[tq:]).astype(o_ref.dtype))


def _sb_attention(qkv, tq=LANES, nblk=2):
    b, s, _ = qkv.shape
    width = SB_HEADS * SB_HEAD_DIM
    pairs = width // LANES
    j = np.arange(SB_BAND)
    u2 = (j[:, None] > j[None, :]).astype(np.float32)
    r = np.arange(2 * tq)
    ur = (r[:, None] < r[None, :]).astype(np.float32)
    return pl.pallas_call(
        functools.partial(_sb_attn_kernel, tq=tq, dh=SB_HEAD_DIM, scale=SB_HEAD_DIM ** -0.5, pairs=pairs,
                          nblk=nblk),
        grid=(b, s // (nblk * tq)),
        in_specs=[pl.BlockSpec((1, nblk * tq, width), lambda bi, i: (bi, i, 0)),
                  pl.BlockSpec((1, s, width), lambda bi, i: (bi, 0, 1)),
                  pl.BlockSpec((1, s, width), lambda bi, i: (bi, 0, 2)),
                  pl.BlockSpec(u2.shape, lambda bi, i: (0, 0)),
                  pl.BlockSpec(ur.shape, lambda bi, i: (0, 0))],
        out_specs=pl.BlockSpec((1, nblk * tq, width), lambda bi, i: (bi, i, 0)),
        out_shape=jax.ShapeDtypeStruct((b, s, width), BF16),
        scratch_shapes=[pltpu.VMEM((nblk, pairs, 2 * tq, LANES), F32), pltpu.VMEM((nblk, pairs, 2 * tq, 1), F32),
                        pltpu.VMEM((SB_TAIL_ROWS, width), F32), pltpu.VMEM((SB_TAIL_ROWS, 1), F32)],
        compiler_params=_params(("arbitrary", "arbitrary"), 48),
        name="sb_attention",
    )(qkv, qkv, qkv, jnp.asarray(u2, BF16), jnp.asarray(ur, BF16))


HG_LEVELS = int(math.log2(HG_CHUNK))
HG_DIAG = HG_LEVELS


def _hgrn_tables():
    c = HG_CHUNK
    t = np.arange(c)[:, None]
    j = np.arange(c)[None, :]
    mats = [(j <= t), (j > t)]
    for lv in range(HG_LEVELS):
        m = 1 << lv
        mid = (t // (2 * m)) * (2 * m) + m - 1
        later = (t % (2 * m)) >= m
        mats.append(np.where(later, (j > mid) & (j <= t), (j > t) & (j <= mid)))
    stack = np.concatenate(mats, axis=0).astype(np.float32)
    s = j
    level = np.full((c, c), -1, np.int32)
    for lv in range(HG_LEVELS):
        level[(t > s) & ((t ^ s) >> lv == 1)] = lv
    level[t == s] = HG_DIAG
    return stack, level


def _hgrn_kernel(q_ref, f_ref, i_ref, g_ref, lbl_ref, gain_ref, gs_ref, lvl_ref, o_ref,
                 st_ref, ex_ref, sc_ref, kv_ref, acc_ref, *, layer, nch):
    ch = HG_CHUNK

    @pl.when(pl.program_id(1) == 0)
    def _():
        st_ref[...] = jnp.zeros_like(st_ref)

    logits = lbl_ref[...]
    ex = jnp.exp(logits - jnp.max(logits, axis=0, keepdims=True))
    lb = jnp.sum(ex[:layer + 1], axis=0, keepdims=True) / jnp.sum(ex, axis=0, keepdims=True)
    gs = gs_ref[...]
    lvl = lvl_ref[...]
    tiles = [(c, h, slice(c * ch, (c + 1) * ch), slice(h * HG_DIM, (h + 1) * HG_DIM))
             for c in range(nch) for h in range(HG_HEADS)]

    f = f_ref[0]
    e = jnp.exp(-jnp.abs(f))
    inv = 1.0 / (1.0 + e)
    sig_f = jnp.where(f >= 0, inv, e * inv)
    sig_nf = jnp.where(f >= 0, e * inv, inv)
    log_hi, log_lo = _split_bf16(jnp.log(lb + (1.0 - lb) * sig_f), 2)
    key = (1.0 - lb) * sig_nf
    for c in range(nch):
        rows = slice(c * ch, (c + 1) * ch)
        ex_c = jnp.exp(_dot(gs, log_hi[rows]) + _dot(gs, log_lo[rows]))
        for g in range(2 + HG_LEVELS):
            ex_ref[g, rows, :] = ex_c[g * ch:(g + 1) * ch]
    qa = _silu(q_ref[0])
    vb = i_ref[0].astype(BF16)
    cum_e = ex_ref[0]
    qc = (qa * cum_e).astype(BF16)
    kd = (key * ex_ref[1]).astype(BF16)
    qk = qa * key
    for c, h, rows, hs in tiles:
        sc_ref[c, h] = jnp.where(lvl == HG_DIAG, jnp.sum(qk[rows, hs], axis=1, keepdims=True), 0.0)
        kv_ref[c, h] = _dot(vb[rows, hs], kd[rows, hs], TN_DIMS)
    for lv in range(HG_LEVELS):
        p = ex_ref[2 + lv]
        qe = (qa * p).astype(BF16)
        ke = (key * p).astype(BF16)
        for c, h, rows, hs in tiles:
            s_lv = _dot(qe[rows, hs], ke[rows, hs], NT_DIMS)
            sc_ref[c, h] = jnp.where(lvl == lv, s_lv, sc_ref[c, h])
    for c, h, rows, hs in tiles:
        acc_ref[rows, hs] = _dot(sc_ref[c, h].astype(BF16), vb[rows, hs])
    for c, h, rows, hs in tiles:
        st = st_ref[h]
        kv_c = kv_ref[c, h]
        kv_ref[c, h] = st
        st_ref[h] = st * cum_e[(c + 1) * ch - 1:(c + 1) * ch, hs] + kv_c
    for c, h, rows, hs in tiles:
        acc_ref[rows, hs] += _dot(qc[rows, hs], kv_ref[c, h].astype(BF16), NT_DIMS)
    gate = gain_ref[...] * _silu(g_ref[0])
    for h in range(HG_HEADS):
        hs = slice(h * HG_DIM, (h + 1) * HG_DIM)
        o_ref[0, :, hs] = (_rms_scale(acc_ref[:, hs]) * gate[:, hs]).astype(o_ref.dtype)


def _hgrn2(hg, lb_logits, gain, layer, tb=512):
    b, s, _ = hg.shape
    w = HG_HEADS * HG_DIM
    stack, level = _hgrn_tables()
    blk = lambda k: pl.BlockSpec((1, tb, w), lambda bi, i: (bi, i, k))
    nch = tb // HG_CHUNK
    return pl.pallas_call(
        functools.partial(_hgrn_kernel, layer=layer, nch=nch),
        grid=(b, s // tb),
        in_specs=[blk(0), blk(1), blk(2), blk(3),
                  pl.BlockSpec(lb_logits.shape, lambda bi, i: (0, 0)),
                  pl.BlockSpec((1, w), lambda bi, i: (0, 0)),
                  pl.BlockSpec(stack.shape, lambda bi, i: (0, 0)),
                  pl.BlockSpec(level.shape, lambda bi, i: (0, 0))],
        out_specs=pl.BlockSpec((1, tb, w), lambda bi, i: (bi, i, 0)),
        out_shape=jax.ShapeDtypeStruct((b, s, w), BF16),
        scratch_shapes=[pltpu.VMEM((HG_HEADS, HG_DIM, HG_DIM), F32),
                        pltpu.VMEM((2 + HG_LEVELS, tb, w), F32),
                        pltpu.VMEM((nch, HG_HEADS, HG_CHUNK, HG_CHUNK), F32),
                        pltpu.VMEM((nch, HG_HEADS, HG_DIM, HG_DIM), F32),
                        pltpu.VMEM((tb, w), F32)],
        compiler_params=_params(("arbitrary", "arbitrary"), 40),
        name="hgrn2",
    )(hg, hg, hg, hg, lb_logits, gain, jnp.asarray(stack, BF16), jnp.asarray(level))


def _ffn_chunks(dff, width=512):
    edges = list(range(0, dff, width)) + [dff]
    return tuple(zip(edges[:-1], edges[1:]))


def _ffn_residual(x1, mod_ref, gain_ref, wg_ref, wu_ref, wo_ref, a_ref):
    h = _norm_mod(x1, gain_ref[...], mod_ref[0, 4:5, :], mod_ref[0, 3:4, :]).astype(BF16)
    for c0, c1 in _ffn_chunks(wg_ref.shape[1]):
        g = _dot(h, wg_ref[:, c0:c1])
        u = _dot(h, wu_ref[:, c0:c1])
        a_ref[:, c0:c1] = (_silu(g) * u).astype(BF16)
    return x1 + mod_ref[0, 5:6, :] * _dot(a_ref[...], wo_ref[...])


def _hybrid_out_ffn_kernel(x_ref, oa_ref, ob_ref, mod_ref, gain_ref, wa_ref, wb_ref, wg_ref, wu_ref, wo_ref,
                           nmod_ref, ngain_ref, nw_ref, o_ref, u_ref, a_ref):
    mix = _dot(oa_ref[0], wa_ref[...]) + _dot(ob_ref[0], wb_ref[...])
    x1 = x_ref[0] + mod_ref[0, 2:3, :] * mix
    x2 = _ffn_residual(x1, mod_ref, gain_ref, wg_ref, wu_ref, wo_ref, a_ref)
    o_ref[0] = x2
    h = _norm_mod(x2, ngain_ref[...], nmod_ref[0, 1:2, :], nmod_ref[0, 0:1, :]).astype(BF16)
    u_ref[0] = _dot(h, nw_ref[...])


def _hybrid_out_ffn(x, o_a, o_b, mod, gain_f, w_a, w_b, wg, wu, wo, next_mod, next_gain_m, next_w_in, tm=512):
    b, s, d = x.shape
    dff = wg.shape[1]
    row = lambda n: pl.BlockSpec((1, tm, n), lambda bi, i: (bi, i, 0))
    const = lambda shape: _resident(shape, lambda bi, i: (0, 0))
    per_batch = pl.BlockSpec((1, N_MOD, d), lambda bi, i: (bi, 0, 0))
    vec = pl.BlockSpec((1, d), lambda bi, i: (0, 0))
    return pl.pallas_call(
        _hybrid_out_ffn_kernel,
        grid=(b, s // tm),
        in_specs=[row(d), row(o_a.shape[2]), row(o_b.shape[2]), per_batch, vec,
                  const(w_a.shape), const(w_b.shape), const(wg.shape), const(wu.shape), const(wo.shape),
                  per_batch, vec, const(next_w_in.shape)],
        out_specs=[row(d), row(d)],
        out_shape=[jax.ShapeDtypeStruct((b, s, d), F32), jax.ShapeDtypeStruct((b, s, d), F32)],
        scratch_shapes=[pltpu.VMEM((tm, dff), BF16)],
        compiler_params=_params(("arbitrary", "arbitrary"), 56),
        name="hybrid_out_ffn",
    )(x, o_a, o_b, mod, gain_f, w_a, w_b, wg, wu, wo, next_mod, next_gain_m, next_w_in)


def _s5_glu_ffn_kernel(x_ref, y_ref, mod_ref, gain_ref, fin_ref, wv_ref, wt_ref, wg_ref, wu_ref, wo_ref,
                       o_ref, a_ref):
    y = y_ref[0]
    mix = _dot(y, wv_ref[...]) * _sigmoid(_dot(y, wt_ref[...]))
    x1 = x_ref[0] + mod_ref[0, 2:3, :] * mix
    x2 = _ffn_residual(x1, mod_ref, gain_ref, wg_ref, wu_ref, wo_ref, a_ref)
    o_ref[0] = _rms_scale(x2) * fin_ref[...]


def _s5_glu_ffn(x, y, mod, gain_f, final_g, wv, wt, wg, wu, wo, tm=512):
    b, s, d = x.shape
    dff = wg.shape[1]
    const = lambda shape: _resident(shape, lambda bi, i: (0, 0))
    return pl.pallas_call(
        _s5_glu_ffn_kernel,
        grid=(b, s // tm),
        in_specs=[pl.BlockSpec((1, tm, d), lambda bi, i: (bi, i, 0)),
                  pl.BlockSpec((1, tm, d), lambda bi, i: (bi, i, 0)),
                  pl.BlockSpec((1, N_MOD, d), lambda bi, i: (bi, 0, 0)),
                  pl.BlockSpec((1, d), lambda bi, i: (0, 0)),
                  pl.BlockSpec((1, d), lambda bi, i: (0, 0)),
                  const(wv.shape), const(wt.shape), const(wg.shape), const(wu.shape), const(wo.shape)],
        out_specs=pl.BlockSpec((1, tm, d), lambda bi, i: (bi, i, 0)),
        out_shape=jax.ShapeDtypeStruct((b, s, d), F32),
        scratch_shapes=[pltpu.VMEM((tm, dff), BF16)],
        compiler_params=_params(("arbitrary", "arbitrary"), 56),
        name="s5_glu_ffn",
    )(x, y, mod, gain_f, final_g, wv, wt, wg, wu, wo)


def _s5_discretise_kernel(lr_ref, li_ref, ldt_ref, br_ref, bi_ref, are_ref, aim_ref, bbr_ref, bbi_ref):
    lr = jnp.minimum(lr_ref[...], -1e-4)
    li = li_ref[...]
    dt = jnp.exp(ldt_ref[...])
    mag = jnp.exp(lr * dt)
    a_re = mag * jnp.cos(li * dt)
    a_im = mag * jnp.sin(li * dt)
    e_re = a_re - 1.0
    den = lr * lr + li * li
    z_re = (e_re * lr + a_im * li) / den
    z_im = (a_im * lr - e_re * li) / den
    are_ref[...] = a_re
    aim_ref[...] = a_im
    br = br_ref[...]
    bi = bi_ref[...]
    bbr_ref[...] = z_re[:, None, :] * br - z_im[:, None, :] * bi
    bbi_ref[...] = z_re[:, None, :] * bi + z_im[:, None, :] * br


def _s5_discretise(lam_re, lam_im, log_dt, b_re_t, b_im_t):
    g, p = lam_re.shape
    return pl.pallas_call(
        _s5_discretise_kernel,
        out_shape=[jax.ShapeDtypeStruct((g, p), F32), jax.ShapeDtypeStruct((g, p), F32),
                   jax.ShapeDtypeStruct(b_re_t.shape, F32), jax.ShapeDtypeStruct(b_re_t.shape, F32)],
        name="s5_discretise",
    )(lam_re, lam_im, log_dt.reshape(g, 1), b_re_t, b_im_t)


def _gelu_tanh(x):
    return 0.5 * x * (1.0 + jnp.tanh(math.sqrt(2.0 / math.pi) * (x + 0.044715 * (x * x * x))))


def _s5_scan_kernel(u_ref, bm_ref, cr_ref, ci_ref, are_ref, aim_ref, d_ref, o_ref,
                    u_sc, ub_sc, bu_ref, xr_ref, xi_ref, y_ref, st_ref, *, ts, nb, half, sub):
    @pl.when(pl.program_id(1) == 0)
    def _():
        st_ref[...] = jnp.zeros_like(st_ref)

    tsub = ts // sub
    rows = tsub * nb
    n_col = 2 * half // MXU_DIM
    n_k = half // MXU_DIM
    u_sc[...] = jnp.swapaxes(u_ref[...], 0, 1).reshape(ts * nb, LANES)
    ub_sc[...] = u_sc[...].astype(BF16)
    a_re = jnp.broadcast_to(are_ref[0], (nb, half))
    a_im = jnp.broadcast_to(aim_ref[0], (nb, half))

    def bu_tile(j, n):
        cols = slice(n * MXU_DIM, (n + 1) * MXU_DIM)
        bu_ref[j % 2, :, cols] = _dot(ub_sc[j * rows:(j + 1) * rows, :], bm_ref[0, :, cols])

    def y_tile(j, n):
        ks = slice((n % n_k) * MXU_DIM, (n % n_k + 1) * MXU_DIM)
        if n < n_k:
            part = _dot(xr_ref[j % 2, :, ks].astype(BF16), cr_ref[0, ks, :])
            y_ref[...] = part if n == 0 else y_ref[...] + part
        else:
            y_ref[...] -= _dot(xi_ref[j % 2, :, ks].astype(BF16), ci_ref[0, ks, :])

    def finish(j):
        y = _gelu_tanh(y_ref[...] + d_ref[0] * u_sc[j * rows:(j + 1) * rows, :])
        o_ref[:, j * tsub:(j + 1) * tsub, :] = jnp.swapaxes(y.reshape(tsub, nb, LANES), 0, 1).astype(o_ref.dtype)

    def scan_piece(j, n, state):
        x_re, x_im = state
        for t in range(n * tsub // n_col, (n + 1) * tsub // n_col):
            r = slice(t * nb, (t + 1) * nb)
            x_re, x_im = (a_re * x_re - a_im * x_im + bu_ref[j % 2, r, 0:half],
                          a_re * x_im + a_im * x_re + bu_ref[j % 2, r, half:2 * half])
            xr_ref[j % 2, r, :] = x_re
            xi_ref[j % 2, r, :] = x_im
        return x_re, x_im

    for n in range(n_col):
        bu_tile(0, n)
    state = (st_ref[:, 0:half], st_ref[:, half:2 * half])
    for j in range(sub):
        for n in range(n_col):
            state = scan_piece(j, n, state)
            if j + 1 < sub:
                bu_tile(j + 1, n)
            if j >= 1:
                y_tile(j - 1, n)
        if j >= 1:
            finish(j - 1)
    for n in range(n_col):
        y_tile(sub - 1, n)
    finish(sub - 1)
    st_ref[:, 0:half] = state[0]
    st_ref[:, half:2 * half] = state[1]


def _s5_scan(u, bmat, cr, ci, a_re, a_im, d_skip, ts=512, sub=4):
    nb, s, d = u.shape
    slabs = d // LANES
    half = bmat.shape[2] // 2
    rows = ts // sub * nb
    io = pl.BlockSpec((nb, ts, LANES), lambda o, i: (0, i, o))
    per_slab = lambda shape: pl.BlockSpec((1,) + shape, lambda o, i: (o, 0, 0))
    return pl.pallas_call(
        functools.partial(_s5_scan_kernel, ts=ts, nb=nb, half=half, sub=sub),
        grid=(slabs, s // ts),
        in_specs=[io, per_slab(bmat.shape[1:]), per_slab(cr.shape[1:]), per_slab(ci.shape[1:]),
                  per_slab((1, half)), per_slab((1, half)), per_slab((1, LANES))],
        out_specs=io,
        out_shape=jax.ShapeDtypeStruct((nb, s, d), BF16),
        scratch_shapes=[pltpu.VMEM((ts * nb, LANES), F32), pltpu.VMEM((ts * nb, LANES), BF16),
                        pltpu.VMEM((2, rows, 2 * half), F32),
                        pltpu.VMEM((2, rows, half), F32), pltpu.VMEM((2, rows, half), F32),
                        pltpu.VMEM((rows, LANES), F32), pltpu.VMEM((nb, 2 * half), F32)],
        compiler_params=_params(("arbitrary", "arbitrary"), 48),
        name="s5_scan",
    )(u, bmat, cr, ci, a_re, a_im, d_skip)


def _block_diag_slabs(t, per_slab):
    g, a, b = t.shape
    t = t.reshape(g // per_slab, per_slab, a, b)
    eye = jnp.eye(per_slab, dtype=t.dtype)
    return jnp.einsum('ogab,gk->ogakb', t, eye).reshape(g // per_slab, per_slab * a, per_slab * b)


def kernel(x, c, norm_mix_g, norm_ffn_g, ada_w, ada_b, ffn_w_in, ffn_w_out, final_norm_g, hy_w_in, hy_w_out, hg_norm_g, hg_lb_logits, s5_w_in, s5_lam_re, s5_lam_im, s5_log_dt, s5_b_re, s5_b_im, s5_c_re, s5_c_im, s5_d, s5_w_glu):
    b, s, d = x.shape
    depth = ada_w.shape[0]
    dff = ffn_w_out.shape[1]
    sb_w = SB_HEADS * SB_HEAD_DIM
    mod = _adaln_mod(c, ada_w, ada_b).reshape(depth, b, N_MOD, d)
    per_slab = LANES // S5_GROUP

    for layer in range(depth):
        i = layer // 2
        wg = ffn_w_in[layer, :, :dff].astype(BF16)
        wu = ffn_w_in[layer, :, dff:].astype(BF16)
        wo = ffn_w_out[layer].astype(BF16)
        gain_m = norm_mix_g[layer].reshape(1, d)
        gain_f = norm_ffn_g[layer].reshape(1, d)
        if layer % 2 == 0:
            assert layer + 1 < depth, "the hybrid layer's last kernel also projects the following S5 layer's input"
            qkv, hg = _hybrid_in_proj(x, mod[layer], gain_m, hy_w_in[i].astype(BF16))
            o_a = _sb_attention(qkv)
            o_b = _hgrn2(hg, hg_lb_logits, hg_norm_g[i].reshape(1, -1), layer)
            w_out = hy_w_out[i].astype(BF16)
            x, u = _hybrid_out_ffn(x, o_a, o_b, mod[layer], gain_f, w_out[:sb_w], w_out[sb_w:], wg, wu, wo,
                                   mod[layer + 1], norm_mix_g[layer + 1].reshape(1, d), s5_w_in[i].astype(BF16))
        else:
            a_re, a_im, bb_re, bb_im = _s5_discretise(
                s5_lam_re[i], s5_lam_im[i], s5_log_dt[i],
                jnp.swapaxes(s5_b_re[i], 1, 2), jnp.swapaxes(s5_b_im[i], 1, 2))
            slabs = d // LANES
            bmat = jnp.concatenate([_block_diag_slabs(bb_re, per_slab), _block_diag_slabs(bb_im, per_slab)],
                                   axis=2).astype(BF16)
            cr = _block_diag_slabs(jnp.swapaxes(s5_c_re[i], 1, 2), per_slab).astype(BF16)
            ci = _block_diag_slabs(jnp.swapaxes(s5_c_im[i], 1, 2), per_slab).astype(BF16)
            half = per_slab * S5_STATE
            y = _s5_scan(u, bmat, cr, ci, a_re.reshape(slabs, 1, half), a_im.reshape(slabs, 1, half),
                         s5_d[i].reshape(slabs, 1, LANES))
            w_glu = s5_w_glu[i].astype(BF16)
            last = layer == depth - 1
            x = _s5_glu_ffn(x, y, mod[layer], gain_f, final_norm_g.reshape(1, d),
                            w_glu[:, :d], w_glu[:, d:], wg, wu, wo)
            assert last, "final norm is fused into the S5 layer's FFN kernel"
    return x
```
